```python
import math
import jax, jax.numpy as jnp
from jax import lax
import numpy as np

D_MODEL = 1024
BATCH = 1
SEQ = 16384
DEPTH = 1
DEC_BATCH = 32
DEC_SEQ = 1
PAST_LEN = 16384
PAGE_SIZE = 128

GDN_HEADS = 8
GDN_DK = 64
GDN_DV = 64
CONV_WIDTH = 4
CHUNK = 64
ATT_HEADS = 8
ATT_DH = 64
IDX_HEADS = 8
IDX_DH = 64
TOPK_MAX = 256
Q_BLOCK = 128
N_GROUPS = 4
EXPERTS_PER_GROUP = 8
N_EXPERTS = N_GROUPS * EXPERTS_PER_GROUP
TOP_K_EXPERT = 2
D_EXPERT = 256
EPS = 1e-6

GDN_QK = GDN_HEADS * GDN_DK
GDN_V = GDN_HEADS * GDN_DV
CONV_DIM = 2 * GDN_QK + GDN_V
ATT_W = ATT_HEADS * ATT_DH
MIX_W = GDN_V + ATT_W
IN_SPLITS = (CONV_DIM, GDN_V, GDN_HEADS, GDN_HEADS, ATT_W, ATT_W, ATT_W, IDX_HEADS * IDX_DH, IDX_DH, IDX_HEADS)
IN_W = sum(IN_SPLITS)
F32 = jnp.float32

kernel_name = "hymba_gdn_dsa_hmoe_step"


def split_cols(a, sizes):
    offs = np.cumsum(np.array(sizes))[:-1].tolist()
    return jnp.split(a, offs, axis=-1)


def rms_norm(x, g):
    xf = x.astype(F32)
    y = xf * lax.rsqrt(jnp.mean(xf * xf, axis=-1, keepdims=True) + EPS)
    return (y * g.astype(F32)).astype(x.dtype)


def l2norm(x):
    return x * lax.rsqrt(jnp.sum(x * x, axis=-1, keepdims=True) + EPS)


def gdn_prepare(qkv_raw, conv_buf, conv_w, a, b, A_log, dt_bias):
    Bn, T, _ = qkv_raw.shape
    ext = jnp.concatenate([conv_buf.astype(qkv_raw.dtype), qkv_raw], axis=1)
    conv = ext[:, 0:T] * conv_w[0]
    for i in range(1, CONV_WIDTH):
        conv = conv + ext[:, i:i + T] * conv_w[i]
    qkv = jax.nn.silu(conv)
    new_buf = ext[:, T:]
    q, k, v = split_cols(qkv, (GDN_QK, GDN_QK, GDN_V))
    q = l2norm(q.reshape(Bn, T, GDN_HEADS, GDN_DK).astype(F32)) * (GDN_DK ** -0.5)
    k = l2norm(k.reshape(Bn, T, GDN_HEADS, GDN_DK).astype(F32))
    v = v.reshape(Bn, T, GDN_HEADS, GDN_DV).astype(F32)
    beta = jax.nn.sigmoid(b.astype(F32))
    g = -jnp.exp(A_log.astype(F32)) * jax.nn.softplus(a.astype(F32) + dt_bias.astype(F32))
    return q, k, v, g, beta, new_buf


def gated_delta_chunked(q, k, v, g, beta, s0):
    Bn, T, H, dk = q.shape
    n = T // CHUNK

    def chunks(a):
        a = a.reshape(Bn, n, CHUNK, H, *a.shape[3:])
        return jnp.moveaxis(jnp.moveaxis(a, 1, 0), 3, 2)

    qc, kc, vc, bc = chunks(q), chunks(k), chunks(v), chunks(beta)
    gc = jnp.cumsum(chunks(g), axis=-1)
    causal = jnp.tril(jnp.ones((CHUNK, CHUNK), bool))
    strict = jnp.tril(jnp.ones((CHUNK, CHUNK), bool), -1)
    decay = jnp.exp(jnp.where(causal, gc[..., :, None] - gc[..., None, :], -jnp.inf))
    kb = kc * bc[..., None]
    a_mat = jnp.where(strict, jnp.einsum('nbhid,nbhjd->nbhij', kb, kc) * decay, 0.0)
    eye = jnp.eye(CHUNK, dtype=F32)
    t_mat = lax.linalg.triangular_solve(a_mat + eye, jnp.broadcast_to(eye, a_mat.shape),
                                        left_side=True, lower=True, unit_diagonal=True)
    u = t_mat @ (vc * bc[..., None])
    w = t_mat @ (kb * jnp.exp(gc)[..., None])

    def step(S, xs):
        qi, ki, ui, wi, gi, di = xs
        v_new = ui - jnp.einsum('bhcd,bhde->bhce', wi, S)
        attn = jnp.einsum('bhid,bhjd->bhij', qi, ki) * di
        o = (jnp.einsum('bhcd,bhde->bhce', qi * jnp.exp(gi)[..., None], S)
             + jnp.einsum('bhij,bhje->bhie', attn, v_new))
        g_last = gi[..., -1:]
        S = S * jnp.exp(g_last)[..., None] + jnp.einsum(
            'bhcd,bhce->bhde', ki * jnp.exp(g_last - gi)[..., None], v_new)
        return S, o

    S, o = lax.scan(step, s0, (qc, kc, u, w, gc, decay))
    o = jnp.moveaxis(jnp.moveaxis(o, 2, 3), 0, 1).reshape(Bn, T, H, -1)
    return o, S


def gated_delta_recurrent(q, k, v, g, beta, s0):
    def step(S, xs):
        qt, kt, vt, gt, bt = xs
        S = S * jnp.exp(gt)[..., None, None]
        delta = (vt - jnp.einsum('bhd,bhde->bhe', kt, S)) * bt[..., None]
        S = S + jnp.einsum('bhd,bhe->bhde', kt, delta)
        return S, jnp.einsum('bhd,bhde->bhe', qt, S)

    xs = tuple(jnp.moveaxis(a, 1, 0) for a in (q, k, v, g, beta))
    S, o = lax.scan(step, s0, xs)
    return jnp.moveaxis(o, 0, 1), S


def gdn_output(o, z, gdn_norm_g, dtype):
    Bn, T = o.shape[:2]
    zh = z.reshape(Bn, T, GDN_HEADS, GDN_DV).astype(F32)
    o = rms_norm(o, gdn_norm_g) * jax.nn.silu(zh)
    return o.reshape(Bn, T, GDN_V).astype(dtype)


def dsa_prepare(qa, ka, va, qi, ki, wi, q_norm_g, k_norm_g, idx_k_norm_g):
    Bn, T, _ = qa.shape
    q = rms_norm(qa.reshape(Bn, T, ATT_HEADS, ATT_DH), q_norm_g)
    k = rms_norm(ka.reshape(Bn, T, ATT_HEADS, ATT_DH), k_norm_g)
    v = va.reshape(Bn, T, ATT_HEADS, ATT_DH)
    qi = qi.reshape(Bn, T, IDX_HEADS, IDX_DH)
    ki = rms_norm(ki, idx_k_norm_g)
    wi = wi.astype(F32) * ((IDX_HEADS * IDX_DH) ** -0.5)
    return q, k, v, qi, ki, wi


def dsa_attend(q, qi, wi, q_pos, idx_keys, gather_kv, topk):
    L = idx_keys.shape[1]
    s = jax.nn.relu(jnp.einsum('bqhd,bld->bqhl', qi, idx_keys).astype(F32))
    score = jnp.einsum('bqhl,bqh->bql', s, wi)
    visible = jnp.arange(L, dtype=jnp.int32)[None, :] <= q_pos[:, None]
    score = jnp.where(visible[None], score, -jnp.inf)
    _, sel = lax.top_k(score, topk)
    k_sel, v_sel = gather_kv(sel)
    logits = jnp.einsum('bqhd,bqkhd->bqhk', q, k_sel).astype(F32) * (ATT_DH ** -0.5)
    valid = sel <= q_pos[None, :, None]
    logits = jnp.where(valid[:, :, None, :], logits, -jnp.inf)
    p = jax.nn.softmax(logits, axis=-1)
    return jnp.einsum('bqhk,bqkhd->bqhd', p.astype(v_sel.dtype), v_sel)


def hier_moe(x, w_rg, w_re, w_gate, w_up, w_down):
    Bn, T, D = x.shape
    xt = x.reshape(Bn * T, D)
    p_grp = jax.nn.softmax((xt @ w_rg).astype(F32), axis=-1)
    g_prob, g_idx = lax.top_k(p_grp, 1)
    logits_e = jnp.einsum('nd,gde->nge', xt, w_re).astype(F32)
    logits_e = jnp.take_along_axis(logits_e, g_idx[:, :, None], axis=1)[:, 0]
    e_prob, e_idx = lax.top_k(jax.nn.softmax(logits_e, axis=-1), TOP_K_EXPERT)
    gate = g_prob * e_prob / jnp.sum(e_prob, axis=-1, keepdims=True)
    expert = g_idx * EXPERTS_PER_GROUP + e_idx
    combine = jnp.sum(jax.nn.one_hot(expert, N_EXPERTS, dtype=F32) * gate[..., None], axis=1)
    h = jax.nn.silu(jnp.einsum('nd,edf->nef', xt, w_gate)) * jnp.einsum('nd,edf->nef', xt, w_up)
    h = h * combine[..., None].astype(h.dtype)
    return jnp.einsum('nef,efd->nd', h, w_down).reshape(Bn, T, D)


def finish(x, o_g, o_a, w_out, ffn_norm_g, w_rg, w_re, w_gate, w_up, w_down):
    h = x + jnp.concatenate([o_g, o_a], axis=-1) @ w_out
    return h + hier_moe(rms_norm(h, ffn_norm_g), w_rg, w_re, w_gate, w_up, w_down)


def prompt_layer(x, lp):
    (attn_norm_g, w_in, conv_w, A_log, dt_bias, gdn_norm_g, q_norm_g, k_norm_g, idx_k_norm_g,
     w_out, ffn_norm_g, w_rg, w_re, w_gate, w_up, w_down) = lp
    Bn, T, _ = x.shape
    qkv_raw, z, a, b, qa, ka, va, qi, ki, wi = split_cols(rms_norm(x, attn_norm_g) @ w_in, IN_SPLITS)
    zero_buf = jnp.zeros((Bn, CONV_WIDTH - 1, CONV_DIM), x.dtype)
    q, k, v, g, beta, new_buf = gdn_prepare(qkv_raw, zero_buf, conv_w, a, b, A_log, dt_bias)
    s0 = jnp.zeros((Bn, GDN_HEADS, GDN_DK, GDN_DV), F32)
    o_g, s_new = gated_delta_chunked(q, k, v, g, beta, s0)
    o_g = gdn_output(o_g, z, gdn_norm_g, x.dtype)
    qd, kd, vd, qid, kid, wid = dsa_prepare(qa, ka, va, qi, ki, wi, q_norm_g, k_norm_g, idx_k_norm_g)
    topk = min(TOPK_MAX, T // 4)

    def gather(sel):
        take = jax.vmap(lambda arr, s: arr[s])
        return take(kd, sel), take(vd, sel)

    nb = T // Q_BLOCK
    blk = lambda arr: jnp.moveaxis(arr.reshape(Bn, nb, Q_BLOCK, *arr.shape[2:]), 1, 0)
    pos = jnp.arange(T, dtype=jnp.int32).reshape(nb, Q_BLOCK)
    o_a = lax.map(lambda xs: dsa_attend(xs[0], xs[1], xs[2], xs[3], kid, gather, topk),
                  (blk(qd), blk(qid), blk(wid), pos))
    o_a = jnp.moveaxis(o_a, 0, 1).reshape(Bn, T, ATT_W)
    y = finish(x, o_g, o_a, w_out, ffn_norm_g, w_rg, w_re, w_gate, w_up, w_down)
    return y, kd, vd, kid, s_new, new_buf


def sample_layer(x, ck, cv, cik, s0, buf, page_table, lp):
    (attn_norm_g, w_in, conv_w, A_log, dt_bias, gdn_norm_g, q_norm_g, k_norm_g, idx_k_norm_g,
     w_out, ffn_norm_g, w_rg, w_re, w_gate, w_up, w_down) = lp
    Bn, T, _ = x.shape
    qkv_raw, z, a, b, qa, ka, va, qi, ki, wi = split_cols(rms_norm(x, attn_norm_g) @ w_in, IN_SPLITS)
    q, k, v, g, beta, new_buf = gdn_prepare(qkv_raw, buf, conv_w, a, b, A_log, dt_bias)
    o_g, s_new = gated_delta_recurrent(q, k, v, g, beta, s0.astype(F32))
    o_g = gdn_output(o_g, z, gdn_norm_g, x.dtype)
    qd, kd, vd, qid, kid, wid = dsa_prepare(qa, ka, va, qi, ki, wi, q_norm_g, k_norm_g, idx_k_norm_g)
    past = page_table.shape[1] * PAGE_SIZE
    past_idx = cik[page_table].reshape(Bn, past, IDX_DH).astype(kid.dtype)
    idx_keys = jnp.concatenate([past_idx, kid], axis=1)
    topk = min(TOPK_MAX, (past + T) // 4)

    def gather(sel):
        in_past = sel < past
        sp = jnp.minimum(sel, past - 1)
        phys = jnp.take_along_axis(page_table, (sp // PAGE_SIZE).reshape(Bn, -1), axis=1).reshape(sel.shape)
        slot = sp % PAGE_SIZE
        sn = jnp.clip(sel - past, 0, T - 1)
        take = jax.vmap(lambda arr, s: arr[s])
        m = in_past[..., None, None]
        k_sel = jnp.where(m, ck[phys, slot].astype(kd.dtype), take(kd, sn))
        v_sel = jnp.where(m, cv[phys, slot].astype(vd.dtype), take(vd, sn))
        return k_sel, v_sel

    pos = past + jnp.arange(T, dtype=jnp.int32)
    o_a = dsa_attend(qd, qid, wid, pos, idx_keys, gather, topk).reshape(Bn, T, ATT_W)
    y = finish(x, o_g, o_a, w_out, ffn_norm_g, w_rg, w_re, w_gate, w_up, w_down)
    return y, kd, vd, kid, s_new, new_buf


def setup_inputs(seed: int = 0) -> dict:
    key = jax.random.key(seed)
    ks = jax.random.split(key, 32)
    n_pages = PAST_LEN // PAGE_SIZE
    n_pool = (DEC_BATCH * n_pages * 5) // 4
    nrm = lambda kk, shape, s=1.0: s * jax.random.normal(kk, shape, F32)
    gain = lambda kk, shape: 1.0 + nrm(kk, shape, 0.1)
    page_table = jax.random.permutation(ks[7], n_pool)[:DEC_BATCH * n_pages].reshape(
        DEC_BATCH, n_pages).astype(jnp.int32)
    dt = jnp.exp(jax.random.uniform(ks[12], (DEPTH, GDN_HEADS), F32, math.log(1e-3), math.log(1e-1)))
    return {
        'x_prompt': nrm(ks[0], (BATCH, SEQ, D_MODEL)),
        'x_sample': nrm(ks[1], (DEC_BATCH, DEC_SEQ, D_MODEL)),
        'cache_k': nrm(ks[2], (DEPTH, n_pool, PAGE_SIZE, ATT_HEADS, ATT_DH)),
        'cache_v': nrm(ks[3], (DEPTH, n_pool, PAGE_SIZE, ATT_HEADS, ATT_DH)),
        'cache_idx_k': nrm(ks[4], (DEPTH, n_pool, PAGE_SIZE, IDX_DH)),
        'state_gdn': nrm(ks[5], (DEPTH, DEC_BATCH, GDN_HEADS, GDN_DK, GDN_DV), 0.1),
        'state_conv': nrm(ks[6], (DEPTH, DEC_BATCH, CONV_WIDTH - 1, CONV_DIM)),
        'page_table': page_table,
        'attn_norm_g': gain(ks[8], (DEPTH, D_MODEL)),
        'w_in': nrm(ks[9], (DEPTH, D_MODEL, IN_W), D_MODEL ** -0.5),
        'conv_w': nrm(ks[10], (DEPTH, CONV_WIDTH, CONV_DIM), 0.5),
        'A_log': jnp.log(jax.random.uniform(ks[11], (DEPTH, GDN_HEADS), F32, 1.0, 16.0)),
        'dt_bias': dt + jnp.log(-jnp.expm1(-dt)),
        'gdn_norm_g': gain(ks[13], (DEPTH, GDN_DV)),
        'q_norm_g': gain(ks[14], (DEPTH, ATT_DH)),
        'k_norm_g': gain(ks[15], (DEPTH, ATT_DH)),
        'idx_k_norm_g': gain(ks[16], (DEPTH, IDX_DH)),
        'w_out': nrm(ks[17], (DEPTH, MIX_W, D_MODEL), MIX_W ** -0.5),
        'ffn_norm_g': gain(ks[18], (DEPTH, D_MODEL)),
        'w_router_group': nrm(ks[19], (DEPTH, D_MODEL, N_GROUPS), D_MODEL ** -0.5),
        'w_router_expert': nrm(ks[20], (DEPTH, N_GROUPS, D_MODEL, EXPERTS_PER_GROUP), D_MODEL ** -0.5),
        'w_gate': nrm(ks[21], (DEPTH, N_EXPERTS, D_MODEL, D_EXPERT), D_MODEL ** -0.5),
        'w_up': nrm(ks[22], (DEPTH, N_EXPERTS, D_MODEL, D_EXPERT), D_MODEL ** -0.5),
        'w_down': nrm(ks[23], (DEPTH, N_EXPERTS, D_EXPERT, D_MODEL), D_EXPERT ** -0.5),
    }


def reference(x_prompt, x_sample, cache_k, cache_v, cache_idx_k, state_gdn, state_conv, page_table,
              attn_norm_g, w_in, conv_w, A_log, dt_bias, gdn_norm_g, q_norm_g, k_norm_g, idx_k_norm_g,
              w_out, ffn_norm_g, w_router_group, w_router_expert, w_gate, w_up, w_down):
    yp, ys = x_prompt, x_sample
    per_layer = []
    for l in range(DEPTH):
        lp = (attn_norm_g[l], w_in[l], conv_w[l], A_log[l], dt_bias[l], gdn_norm_g[l], q_norm_g[l],
              k_norm_g[l], idx_k_norm_g[l], w_out[l], ffn_norm_g[l], w_router_group[l],
              w_router_expert[l], w_gate[l], w_up[l], w_down[l])
        yp, kp, vp, ikp, sp, cp = prompt_layer(yp, lp)
        ys, ksm, vsm, iks, ss, cs = sample_layer(ys, cache_k[l], cache_v[l], cache_idx_k[l], state_gdn[l],
                                                 state_conv[l], page_table, lp)
        per_layer.append((kp, vp, ikp, sp, cp, ksm, vsm, iks, ss, cs))
    cols = [jnp.stack(c, axis=0) for c in zip(*per_layer)]
    k_prompt, v_prompt, idxk_prompt, gdn_prompt, conv_prompt = cols[0], cols[1], cols[2], cols[3], cols[4]
    k_sample, v_sample, idxk_sample, gdn_sample, conv_sample = cols[5], cols[6], cols[7], cols[8], cols[9]
    return (yp, ys, k_prompt, v_prompt, idxk_prompt, gdn_prompt, conv_prompt,
            k_sample, v_sample, idxk_sample, gdn_sample, conv_sample)
```

```python
import functools

import jax
import jax.numpy as jnp
import numpy as np
from jax import lax
from jax.experimental import pallas as pl
from jax.experimental.pallas import tpu as pltpu

F32 = jnp.float32
BF16 = jnp.bfloat16
I32 = jnp.int32
EPS = 1e-6
NEG = -1e30
INT_MIN = -(2 ** 31)

D_MODEL = 1024
HEADS = 8
DH = 64
HW = HEADS * DH
CONV_DIM = 3 * HW
CONV_WIDTH = 4
CHUNK = 64
TOPK_MAX = 256
N_GROUPS = 4
EPG = 8
N_EXPERTS = N_GROUPS * EPG
D_EXPERT = 256
LANES = 128
IN_SPLITS = (CONV_DIM, HW, HEADS, HEADS, HW, HW, HW, HW, DH, HEADS)

TM_IN = 256
TM_FIN = 512
EXPERTS_PER_STEP = 4
BQ = 256
BK = 256
KV_TILES_PER_BLOCK = 8
PAGES_PER_STEP = 16
VMEM_LIMIT = 52 * 1024 * 1024


def _cparams(sem):
    return pltpu.CompilerParams(dimension_semantics=sem, vmem_limit_bytes=VMEM_LIMIT)


def _sigmoid(x):
    return 1.0 / (1.0 + jnp.exp(-x))


def _softplus(x):
    return jnp.maximum(x, 0.0) + jnp.log(1.0 + jnp.exp(-jnp.abs(x)))


def _mm(a, b):
    return jnp.dot(a.astype(BF16), b.astype(BF16), preferred_element_type=F32)


def _mm_nt(a, b):
    return lax.dot_general(a.astype(BF16), b.astype(BF16), (((1,), (1,)), ((), ())),
                           preferred_element_type=F32)


def _split2(x):
    hi = x.astype(BF16)
    return hi, (x - hi.astype(F32)).astype(BF16)


def _mm3(a, b):
    ah, al = _split2(a)
    bh, bl = _split2(b)
    d = lambda x, y: jnp.dot(x, y, preferred_element_type=F32)
    return d(ah, bh) + d(ah, bl) + d(al, bh)


_mm_inv = _mm3


def _mm_nt3(a, b):
    ah, al = _split2(a)
    bh, bl = _split2(b)
    d = lambda x, y: lax.dot_general(x, y, (((1,), (1,)), ((), ())), preferred_element_type=F32)
    return d(ah, bh) + d(ah, bl) + d(al, bh)


def _split3(x):
    x1 = x.astype(BF16)
    r = x - x1.astype(F32)
    x2 = r.astype(BF16)
    x3 = (r - x2.astype(F32)).astype(BF16)
    return x1, x2, x3


def _dot_ones_l(ones_bf, x):
    x1, x2, x3 = _split3(x)
    d = lambda p: jnp.dot(ones_bf, p, preferred_element_type=F32)
    return d(x1) + d(x2) + d(x3)


def _dot_ones_r(x, ones_bf):
    x1, x2, x3 = _split3(x)
    d = lambda p: jnp.dot(p, ones_bf, preferred_element_type=F32)
    return d(x1) + d(x2) + d(x3)


def _head_sum(y, bd):
    return _dot_ones_r(y, bd)


def _inproj_body(x_ref, g_ref, w_ref, bd_ref, qg_ref, kg_ref, ikg_ref,
                 qkv_ref, z_ref, q_ref, k_ref, kb_ref, v_ref, vb_ref, qi_ref, ki_ref, misc_ref, *, exact):
    x = x_ref[...]
    ms = jnp.mean(x * x, axis=-1, keepdims=True)
    h = x * lax.rsqrt(ms + EPS) * g_ref[...]
    if not exact:
        h = h.astype(BF16)

    def seg(lo, n):
        if exact:
            return jnp.dot(h, w_ref[:, lo:lo + n], precision=lax.Precision.HIGHEST, preferred_element_type=F32)
        return jnp.dot(h, w_ref[:, lo:lo + n], preferred_element_type=F32)

    for j in range(3):
        qkv_ref[:, j * HW:(j + 1) * HW] = seg(j * HW, HW)
    z_ref[...] = seg(3 * HW, HW)
    bd = bd_ref[...]

    def head_rms(y, gain):
        return y * lax.rsqrt(_head_sum(y * y, bd) * (1.0 / DH) + EPS) * gain

    q = head_rms(seg(4 * HW, HW), qg_ref[...])
    q_ref[...] = (q * (DH ** -0.5)).astype(q_ref.dtype)
    k = head_rms(seg(5 * HW, HW), kg_ref[...])
    k_ref[...] = k
    kb_ref[...] = k.astype(BF16)
    v = seg(6 * HW, HW)
    v_ref[...] = v
    vb_ref[...] = v.astype(BF16)
    qi_ref[...] = seg(7 * HW, HW).astype(qi_ref.dtype)
    tail = seg(8 * HW, LANES)
    ki = tail[:, :DH]
    ki_ms = jnp.mean(ki * ki, axis=-1, keepdims=True)
    ki_ref[...] = ki * lax.rsqrt(ki_ms + EPS) * ikg_ref[...]
    misc_ref[...] = tail


def _inproj(x_all, attn_g, w_r, bd, qg, kg, ikg, tm, exact):
    tp = x_all.shape[0]
    nw = w_r.shape[1]
    qdt = F32 if exact else BF16
    row = lambda w: pl.BlockSpec((tm, w), lambda i: (i, 0))
    full = lambda a: pl.BlockSpec(a.shape, lambda i: (0,) * a.ndim)
    out_shapes = [
        jax.ShapeDtypeStruct((tp, CONV_DIM), F32),
        jax.ShapeDtypeStruct((tp, HW), F32),
        jax.ShapeDtypeStruct((tp, HW), qdt),
        jax.ShapeDtypeStruct((tp, HW), F32),
        jax.ShapeDtypeStruct((tp, HW), BF16),
        jax.ShapeDtypeStruct((tp, HW), F32),
        jax.ShapeDtypeStruct((tp, HW), BF16),
        jax.ShapeDtypeStruct((tp, HW), qdt),
        jax.ShapeDtypeStruct((tp, DH), F32),
        jax.ShapeDtypeStruct((tp, LANES), F32),
    ]
    out_specs = [row(CONV_DIM), row(HW), row(HW), row(HW), row(HW), row(HW), row(HW), row(HW),
                 row(DH), row(LANES)]
    return pl.pallas_call(
        functools.partial(_inproj_body, exact=exact),
        grid=(tp // tm,),
        in_specs=[row(D_MODEL), full(attn_g), pl.BlockSpec((D_MODEL, nw), lambda i: (0, 0)), full(bd),
                  full(qg), full(kg), full(ikg)],
        out_specs=out_specs,
        out_shape=out_shapes,
        compiler_params=_cparams(("parallel",)),
        name="inproj_exact" if exact else "inproj",
    )(x_all, attn_g, w_r, bd, qg, kg, ikg)


def _gdn_body(qkv_ref, ae_ref, be_ref, at_ref, z_ref, cw_ref, alog_ref, dtb_ref, alogt_ref, dtbt_ref,
              ng_ref, bd_ref, ltri_ref, utri_ref, eye_ref,
              o_ref, s_out_ref, xb, s_scr):
    c = pl.program_id(0)

    @pl.when(c == 0)
    def _():
        xb[0:8, :] = jnp.zeros((8, CONV_DIM), F32)
        s_scr[...] = jnp.zeros_like(s_scr)

    xb[8:8 + CHUNK, :] = qkv_ref[...]
    base = 8 - (CONV_WIDTH - 1)
    conv = xb[base:base + CHUNK, :] * cw_ref[0:1, :]
    for i in range(1, CONV_WIDTH):
        conv = conv + xb[base + i:base + i + CHUNK, :] * cw_ref[i:i + 1, :]
    xb[0:8, :] = xb[CHUNK:CHUNK + 8, :]
    act = conv * _sigmoid(conv)
    q = act[:, 0:HW]
    k = act[:, HW:2 * HW]
    v = act[:, 2 * HW:3 * HW]
    bd = bd_ref[...]
    qn = q * lax.rsqrt(_head_sum(q * q, bd) + EPS) * (DH ** -0.5)
    kn = k * lax.rsqrt(_head_sum(k * k, bd) + EPS)
    beta = _sigmoid(be_ref[...])
    g = -jnp.exp(alog_ref[...]) * _softplus(ae_ref[...] + dtb_ref[...])
    gc = _dot_ones_l(ltri_ref[...], g)
    gt = -jnp.exp(alogt_ref[...]) * _softplus(at_ref[0] + dtbt_ref[...])
    gct = _dot_ones_r(gt, utri_ref[...])
    gl = gc[CHUNK - 1:CHUNK, :]
    eg = jnp.exp(gc)
    kb = kn * beta
    vb = v * beta
    kbe = kb * eg
    qe = qn * eg
    kdec = kn * jnp.exp(gl - gc)
    egl = jnp.exp(gl)
    ri = lax.broadcasted_iota(I32, (CHUNK, CHUNK), 0)
    ci = lax.broadcasted_iota(I32, (CHUNK, CHUNK), 1)
    causal = ri >= ci
    strict = ri > ci
    eye = eye_ref[...]
    heads = range(HEADS)
    hsl = [slice(DH * h, DH * (h + 1)) for h in heads]
    dec = [jnp.exp(jnp.where(causal, gc[:, hsl[h]] - gct[h:h + 1, :], NEG)) for h in heads]
    kh = [kn[:, hsl[h]].astype(BF16) for h in heads]
    p = [-jnp.where(strict, _mm_nt(kb[:, hsl[h]], kh[h]) * dec[h], 0.0) for h in heads]
    attn = [_mm_nt(qn[:, hsl[h]], kh[h]) * dec[h] for h in heads]
    kdt = [_mm_nt(eye, kdec[:, hsl[h]]) for h in heads]
    xx = [jnp.concatenate([vb[:, hsl[h]], kbe[:, hsl[h]]], axis=1) for h in heads]
    for r in range(6):
        xx = [xx[h] + _mm_inv(p[h], xx[h]) for h in heads]
        if r < 5:
            p = [_mm_inv(p[h], p[h]) for h in heads]
    sh = [s_scr[h] for h in heads]
    v_new = [xx[h][:, :DH] - _mm(xx[h][:, DH:], sh[h]) for h in heads]
    outs = [_mm(qe[:, hsl[h]], sh[h]) + _mm(attn[h], v_new[h]) for h in heads]
    for h in heads:
        s_scr[h] = sh[h] * egl[:, hsl[h]] + _mm(kdt[h], v_new[h])
    o = jnp.concatenate(outs, axis=1)
    on = o * lax.rsqrt(_head_sum(o * o, bd) * (1.0 / DH) + EPS) * ng_ref[...]
    z = z_ref[...]
    o_ref[...] = (on * (z * _sigmoid(z))).astype(BF16)

    @pl.when(c == pl.num_programs(0) - 1)
    def _():
        s_out_ref[...] = s_scr[...]


def _gdn_prompt(qkv, a_e, b_e, a_t, z, conv_w, alog_e, dtb_e, alog_t, dtb_t, ng, bd, ltri, utri, eye):
    t = qkv.shape[0]
    row = lambda w: pl.BlockSpec((CHUNK, w), lambda i: (i, 0))
    full = lambda a: pl.BlockSpec(a.shape, lambda i: (0,) * a.ndim)
    return pl.pallas_call(
        _gdn_body,
        grid=(t // CHUNK,),
        in_specs=[row(CONV_DIM), row(HW), row(HW), pl.BlockSpec((1, HEADS, CHUNK), lambda i: (i, 0, 0)), row(HW),
                  full(conv_w), full(alog_e), full(dtb_e), full(alog_t), full(dtb_t), full(ng), full(bd),
                  full(ltri), full(utri), full(eye)],
        out_specs=[row(HW), pl.BlockSpec((HEADS, DH, DH), lambda i: (0, 0, 0))],
        out_shape=[jax.ShapeDtypeStruct((t, HW), BF16), jax.ShapeDtypeStruct((HEADS, DH, DH), F32)],
        scratch_shapes=[pltpu.VMEM((CHUNK + 8, CONV_DIM), F32), pltpu.VMEM((HEADS, DH, DH), F32)],
        compiler_params=_cparams(("arbitrary",)),
        name="gdn_prompt",
    )(qkv, a_e, b_e, a_t, z, conv_w, alog_e, dtb_e, alog_t, dtb_t, ng, bd, ltri, utri, eye)


def _gdn_step_body(xq_ref, xk_ref, xv_ref, cq_ref, ck_ref, cv_ref, a_ref, b_ref, alog_ref, dtb_ref, z_ref, ng_ref,
                   s0_ref, o_ref, s1_ref):
    def conv(x_ref, c_ref):
        acc = x_ref[0, 0] * c_ref[0, 0]
        for i in range(1, CONV_WIDTH):
            acc = acc + x_ref[i, 0] * c_ref[i, 0]
        return acc * _sigmoid(acc)

    q = conv(xq_ref, cq_ref)
    k = conv(xk_ref, ck_ref)
    v = conv(xv_ref, cv_ref)
    qn = q * lax.rsqrt(jnp.sum(q * q, axis=0, keepdims=True) + EPS) * (DH ** -0.5)
    kn = k * lax.rsqrt(jnp.sum(k * k, axis=0, keepdims=True) + EPS)
    beta = _sigmoid(b_ref[0])
    eg = jnp.exp(-jnp.exp(alog_ref[0]) * _softplus(a_ref[0] + dtb_ref[0]))
    ks = jnp.zeros_like(v)
    for d in range(DH):
        ks = ks + kn[d:d + 1, :] * s0_ref[0, d]
    delta = (v - ks * eg) * beta
    o = jnp.zeros_like(v)
    for d in range(DH):
        s_new = s0_ref[0, d] * eg + kn[d:d + 1, :] * delta
        s1_ref[0, d] = s_new
        o = o + qn[d:d + 1, :] * s_new
    on = o * lax.rsqrt(jnp.mean(o * o, axis=0, keepdims=True) + EPS) * ng_ref[...]
    z = z_ref[0]
    o_ref[0] = on * (z * _sigmoid(z))


def _gdn_step(xq, xk, xv, cq, ck, cv, a_t, b_t, alog_t, dtb_t, z_t, ng_t, s0_t):
    b = xq.shape[-1]
    x_spec = pl.BlockSpec((CONV_WIDTH, 1, DH, b), lambda h: (0, h, 0, 0))
    r_spec = pl.BlockSpec((1, 1, b), lambda h: (h, 0, 0))
    s_spec = pl.BlockSpec((1, DH, DH, b), lambda h: (h, 0, 0, 0))
    hd_spec = pl.BlockSpec((1, DH, b), lambda h: (h, 0, 0))
    return pl.pallas_call(
        _gdn_step_body,
        grid=(HEADS,),
        in_specs=[x_spec, x_spec, x_spec, x_spec, x_spec, x_spec, r_spec, r_spec, r_spec, r_spec, hd_spec,
                  pl.BlockSpec((DH, b), lambda h: (0, 0)), s_spec],
        out_specs=[hd_spec, s_spec],
        out_shape=[jax.ShapeDtypeStruct((HEADS, DH, b), F32), jax.ShapeDtypeStruct((HEADS, DH, DH, b), F32)],
        compiler_params=_cparams(("parallel",)),
        name="gdn_step",
    )(xq, xk, xv, cq, ck, cv, a_t, b_t, alog_t, dtb_t, z_t, ng_t, s0_t)


def _count_tiles(sc, n_tiles, pred):
    def body(kt, cnt):
        hit = jnp.where(pred(kt, sc[kt]), 1, 0).astype(I32)
        return cnt + jnp.sum(hit.reshape(BK // 8, 8, BQ), axis=0)

    cnt = lax.fori_loop(0, n_tiles, body, jnp.zeros((8, BQ), I32))
    return jnp.sum(cnt, axis=0, keepdims=True)


def _count_ge(sc, n_tiles, cand):
    return _count_tiles(sc, n_tiles, lambda kt, s: s >= cand)


def _key_to_f32(key):
    f = pltpu.bitcast(jnp.where(key < 0, key ^ jnp.int32(0x7FFFFFFF), key), F32)
    return jnp.where(key <= INT_MIN + 0x7FFFFF, -jnp.inf, f)


def _kth_largest(count_fn, k, shape):
    lo = jnp.where(count_fn(jnp.zeros(shape, F32)) >= k, 0, INT_MIN).astype(I32)

    def bit(i, lo):
        cand = lo + jnp.left_shift(jnp.int32(1), 30 - i)
        return jnp.where(count_fn(_key_to_f32(cand)) >= k, cand, lo)

    return _key_to_f32(lax.fori_loop(0, 31, bit, lo))


def _dsa_body(kib_ref, qit_ref, wt_ref, qt_ref, k_ref, vt_ref, o_ref, sc, thr, m_s, l_s, a_s, lg_s, acc, qz, *,
              topk, idx_bits):
    qb = pl.program_id(0)
    kb = pl.program_id(1)
    q0 = qb * BQ
    n_tiles = (q0 + BQ + BK - 1) // BK
    qpos = q0 + lax.broadcasted_iota(I32, (BK, BQ), 1)
    rowi = lax.broadcasted_iota(I32, (BK, BQ), 0)

    @pl.when(kb == 0)
    def _score():
        w = wt_ref[...]

        def tile(kt, carry):
            ki_t = kib_ref[kt]
            s = jnp.zeros((BK, BQ), F32)
            for h in range(HEADS):
                sh = jnp.dot(ki_t, qit_ref[DH * h:DH * (h + 1), :], preferred_element_type=F32)
                s = s + jnp.maximum(sh, 0.0) * w[h:h + 1, :]
            vis = (kt * BK + rowi) <= qpos
            sc[kt] = jnp.where(vis, s, -jnp.inf)
            return carry

        lax.fori_loop(0, n_tiles, tile, 0)
        t = _kth_largest(functools.partial(_count_ge, sc, n_tiles), topk, (1, BQ))
        thr[...] = t
        over = (_count_ge(sc, n_tiles, t) > topk) & (t > -jnp.inf)

        @pl.when(jnp.max(jnp.where(over, 1, 0)) > 0)
        def _break_ties():
            room = topk - _count_tiles(sc, n_tiles, lambda kt, s: s > t)

            def bit(i, x):
                cand = x + jnp.left_shift(jnp.int32(1), idx_bits - 1 - i)
                below = _count_tiles(sc, n_tiles, lambda kt, s: (s == t) & ((kt * BK + rowi) < cand))
                return jnp.where(below < room, cand, x)

            last_kept = lax.fori_loop(0, idx_bits, bit, jnp.zeros((1, BQ), I32))

            def drop(kt, carry):
                s = sc[kt]
                sc[kt] = jnp.where((s == t) & ((kt * BK + rowi) > last_kept), -jnp.inf, s)
                return carry

            lax.fori_loop(0, n_tiles, drop, 0)

        m_s[...] = jnp.full_like(m_s, NEG)
        l_s[...] = jnp.zeros_like(l_s)
        acc[...] = jnp.zeros_like(acc)
        zero = jnp.zeros((DH, BQ), BF16)
        for h in range(HEADS):
            qh = qt_ref[DH * h:DH * (h + 1), :]
            qz[h, 0:DH, :] = qh if h % 2 == 0 else zero
            qz[h, DH:2 * DH, :] = zero if h % 2 == 0 else qh

    @pl.when(kb * (KV_TILES_PER_BLOCK * BK) < q0 + BQ)
    def _attend():
        t0 = kb * KV_TILES_PER_BLOCK
        th = thr[...]

        def tile(j, carry):
            gt = t0 + j
            msk = (sc[gt] >= th) & ((gt * BK + rowi) <= qpos)
            bias = jnp.where(msk, 0.0, NEG)
            for h in range(HEADS):
                pair = LANES * (h // 2)
                lg = jnp.dot(k_ref[j, :, pair:pair + LANES], qz[h], preferred_element_type=F32) + bias
                lg_s[h] = lg
                m_old = m_s[h]
                m_new = jnp.maximum(m_old, jnp.max(lg, axis=0, keepdims=True))
                a_s[h] = jnp.exp(m_old - m_new)
                m_s[h] = m_new
            for h in range(HEADS):
                hs = slice(DH * h, DH * (h + 1))
                p = jnp.exp(lg_s[h] - m_s[h])
                alpha = a_s[h]
                l_s[h] = alpha * l_s[h] + jnp.sum(p, axis=0, keepdims=True)
                pv = jnp.dot(vt_ref[j, hs, :], p.astype(BF16), preferred_element_type=F32)
                acc[hs, :] = alpha * acc[hs, :] + pv
            return carry

        lax.fori_loop(0, jnp.minimum(KV_TILES_PER_BLOCK, n_tiles - t0), tile, 0)

    @pl.when(kb == pl.num_programs(1) - 1)
    def _finish():
        for h in range(HEADS):
            o_ref[DH * h:DH * (h + 1), :] = acc[DH * h:DH * (h + 1), :] / l_s[h]


def _dsa_prompt(kib3, qit, wt, qt, k3, vt3, topk):
    nkt = kib3.shape[0]
    t = nkt * BK
    nq = t // BQ
    nkb = -(-nkt // KV_TILES_PER_BLOCK)
    kvb = min(KV_TILES_PER_BLOCK, nkt)
    span = kvb * BK

    def kv_idx(qb, kb):
        return (jnp.minimum(kb, (qb * BQ + BQ - 1) // span), 0, 0)

    return pl.pallas_call(
        functools.partial(_dsa_body, topk=topk, idx_bits=(t - 1).bit_length()),
        grid=(nq, nkb),
        in_specs=[pl.BlockSpec((nkt, BK, DH), lambda qb, kb: (0, 0, 0)),
                  pl.BlockSpec((HW, BQ), lambda qb, kb: (0, qb)),
                  pl.BlockSpec((HEADS, BQ), lambda qb, kb: (0, qb)),
                  pl.BlockSpec((HW, BQ), lambda qb, kb: (0, qb)),
                  pl.BlockSpec((kvb, BK, HW), kv_idx),
                  pl.BlockSpec((kvb, HW, BK), kv_idx)],
        out_specs=pl.BlockSpec((HW, BQ), lambda qb, kb: (0, qb)),
        out_shape=jax.ShapeDtypeStruct((HW, t), F32),
        scratch_shapes=[pltpu.VMEM((nkt, BK, BQ), F32), pltpu.VMEM((1, BQ), F32),
                        pltpu.VMEM((HEADS, 1, BQ), F32), pltpu.VMEM((HEADS, 1, BQ), F32),
                        pltpu.VMEM((HEADS, 1, BQ), F32), pltpu.VMEM((HEADS, BK, BQ), F32),
                        pltpu.VMEM((HW, BQ), F32), pltpu.VMEM((HEADS, 2 * DH, BQ), BF16)],
        compiler_params=_cparams(("parallel", "arbitrary")),
        name="dsa_prompt",
    )(kib3, qit, wt, qt, k3, vt3)


def _prefix_count(mf, upper, lower_strict):
    within = jnp.dot(mf.astype(BF16), upper, preferred_element_type=F32)
    tot = jnp.broadcast_to(within[:, -1:], within.shape)
    return within + jnp.dot(lower_strict, tot.astype(BF16), preferred_element_type=F32)


def _dsa_select_body(pt_ref, *refs, n_chunks, page, topk):
    pp = PAGES_PER_STEP
    ik_refs = refs[0:pp]
    qi_ref, w_ref, kin_ref, mask_ref, meta_ref, sc = refs[pp:]
    del pt_ref
    c = pl.program_id(1)
    n_pages = n_chunks * pp
    qi = qi_ref[...]
    w = w_ref[...]
    srows = []
    for i in range(pp):
        s8 = _mm3(qi, ik_refs[i][...])
        srows.append(jnp.sum(jnp.maximum(s8, 0.0) * w, axis=0, keepdims=True))
    sc[c] = jnp.concatenate(srows, axis=0)

    @pl.when(c == n_chunks - 1)
    def _select():
        s_new = jnp.sum(qi * kin_ref[...], axis=1, keepdims=True)
        s_new = jnp.sum(jnp.maximum(s_new, 0.0) * w[:, 0:1], axis=0, keepdims=True)
        total = lambda m: jnp.sum(jnp.sum(m, axis=1, keepdims=True), axis=0, keepdims=True)
        s_all = sc[...].reshape(n_pages, page)

        def count(cand):
            return total(jnp.where(s_all >= cand, 1.0, 0.0)) + jnp.where(s_new >= cand, 1.0, 0.0)

        thr = _kth_largest(count, topk, (1, 1))
        ri = lax.broadcasted_iota(I32, (page, page), 0)
        ci = lax.broadcasted_iota(I32, (page, page), 1)
        upper = jnp.where(ri <= ci, 1.0, 0.0).astype(BF16)
        rp = lax.broadcasted_iota(I32, (n_pages, n_pages), 0)
        cp = lax.broadcasted_iota(I32, (n_pages, n_pages), 1)
        lower_strict = jnp.where(cp < rp, 1.0, 0.0).astype(BF16)
        gt = s_all > thr
        eq = s_all == thr
        eqf = jnp.where(eq, 1.0, 0.0)
        room = topk - total(jnp.where(gt, 1.0, 0.0)) - jnp.where(s_new > thr, 1.0, 0.0)
        sel = gt | (eq & (_prefix_count(eqf, upper, lower_strict) <= room))
        new_sel = (s_new > thr) | ((s_new == thr) & (total(eqf) < room))
        mask_ref[...] = jnp.where(sel, 1.0, 0.0)
        meta_ref[...] = jnp.where(new_sel, 1.0, 0.0) * jnp.ones(meta_ref.shape, F32)


def _dsa_select(page_table, cik_t, qi_s, w_s, ki_new, topk):
    b, n_pages = page_table.shape
    page = cik_t.shape[2]
    pp = PAGES_PER_STEP
    n_chunks = n_pages // pp

    def ik_spec(i):
        return pl.BlockSpec((None, DH, page), lambda bb, c, pt: (pt[bb, c * pp + i], 0, 0))

    per_b = lambda r, w: pl.BlockSpec((None, r, w), lambda bb, c, pt: (bb, 0, 0))
    grid_spec = pltpu.PrefetchScalarGridSpec(
        num_scalar_prefetch=1,
        grid=(b, n_chunks),
        in_specs=[ik_spec(i) for i in range(pp)] + [per_b(HEADS, DH), per_b(HEADS, page), per_b(1, DH)],
        out_specs=[per_b(n_pages, page), per_b(8, page)],
        scratch_shapes=[pltpu.VMEM((n_chunks, pp, page), F32)],
    )
    return pl.pallas_call(
        functools.partial(_dsa_select_body, n_chunks=n_chunks, page=page, topk=topk),
        grid_spec=grid_spec,
        out_shape=[jax.ShapeDtypeStruct((b, n_pages, page), F32), jax.ShapeDtypeStruct((b, 8, page), F32)],
        compiler_params=_cparams(("parallel", "arbitrary")),
        name="dsa_select",
    )(page_table, *([cik_t] * pp), qi_s, w_s, ki_new)


def _dsa_attend_body(pt_ref, *refs, n_chunks):
    del pt_ref
    pp = PAGES_PER_STEP
    k_refs = refs[0:pp]
    v_refs = refs[pp:2 * pp]
    mask_ref, new_ref, q_ref, kn_ref, vn_ref, o_ref, qb, m_s, l_s, acc = refs[2 * pp:]
    c = pl.program_id(1)

    @pl.when(c == 0)
    def _init():
        qb[...] = jnp.broadcast_to(q_ref[...], qb.shape)
        m_s[...] = jnp.full_like(m_s, NEG)
        l_s[...] = jnp.zeros_like(l_s)
        acc[...] = jnp.zeros_like(acc)

    for i in range(pp):
        sel_row = mask_ref[i:i + 1, :] > 0.0
        for h in range(HEADS):
            lg = jnp.sum(k_refs[i][h] * qb[h], axis=0, keepdims=True)
            lg = jnp.where(sel_row, lg, NEG)
            m_old = m_s[h]
            m_new = jnp.maximum(m_old, jnp.max(lg, axis=1, keepdims=True))
            alpha = jnp.exp(m_old - m_new)
            p = jnp.exp(lg - m_new)
            l_s[h] = alpha * l_s[h] + p
            acc[h] = alpha * acc[h] + p * v_refs[i][h]
            m_s[h] = m_new

    @pl.when(c == n_chunks - 1)
    def _finish():
        new_sel = new_ref[0:1, 0:1].reshape(1, 1, 1) > 0.0
        lg_new = jnp.where(new_sel, jnp.sum(q_ref[...] * kn_ref[...], axis=1, keepdims=True), NEG)
        m_old = m_s[...][:, :, 0:1]
        m_fin = jnp.maximum(m_old, lg_new)
        scale = jnp.exp(m_old - m_fin)
        p_new = jnp.exp(lg_new - m_fin)
        den = jnp.sum(l_s[...], axis=2, keepdims=True) * scale + p_new
        o_ref[...] = (jnp.sum(acc[...], axis=2, keepdims=True) * scale + p_new * vn_ref[...]) / den


def _dsa_attend(page_table, ck_t, cv_t, mask, new_sel, q_c, k_c, v_c):
    b, n_pages = page_table.shape
    page = ck_t.shape[3]
    pp = PAGES_PER_STEP
    n_chunks = n_pages // pp

    def kv_spec(i):
        return pl.BlockSpec((None, HEADS, DH, page), lambda bb, c, pt: (pt[bb, c * pp + i], 0, 0, 0))

    col = pl.BlockSpec((None, HEADS, DH, 1), lambda bb, c, pt: (bb, 0, 0, 0))
    grid_spec = pltpu.PrefetchScalarGridSpec(
        num_scalar_prefetch=1,
        grid=(b, n_chunks),
        in_specs=([kv_spec(i) for i in range(pp)] + [kv_spec(i) for i in range(pp)]
                  + [pl.BlockSpec((None, pp, page), lambda bb, c, pt: (bb, c, 0)),
                     pl.BlockSpec((None, 8, page), lambda bb, c, pt: (bb, 0, 0)), col, col, col]),
        out_specs=col,
        scratch_shapes=[pltpu.VMEM((HEADS, DH, page), F32), pltpu.VMEM((HEADS, 1, page), F32),
                        pltpu.VMEM((HEADS, 1, page), F32), pltpu.VMEM((HEADS, DH, page), F32)],
    )
    return pl.pallas_call(
        functools.partial(_dsa_attend_body, n_chunks=n_chunks),
        grid_spec=grid_spec,
        out_shape=jax.ShapeDtypeStruct((b, HEADS, DH, 1), F32),
        compiler_params=_cparams(("parallel", "arbitrary")),
        name="dsa_attend",
    )(page_table, *([ck_t] * pp), *([cv_t] * pp), mask, new_sel, q_c, k_c, v_c)


def _route(logits):
    lane = lax.broadcasted_iota(I32, logits.shape, 1)
    mx = lambda a: jnp.max(a, axis=-1, keepdims=True)
    sm = lambda a: jnp.sum(a, axis=-1, keepdims=True)
    first = lambda hit: jnp.min(jnp.where(hit, lane, LANES), axis=-1, keepdims=True)
    gmask = lane < N_GROUPS
    lg = jnp.where(gmask, logits, NEG)
    gex = jnp.where(gmask, jnp.exp(lg - mx(lg)), 0.0)
    gp = gex / sm(gex)
    g_prob = mx(gp)
    g_idx = first(gmask & (gp == g_prob))
    e_lo = N_GROUPS + EPG * g_idx
    emask = (lane >= e_lo) & (lane < e_lo + EPG)
    le = jnp.where(emask, logits, NEG)
    eex = jnp.where(emask, jnp.exp(le - mx(le)), 0.0)
    ep = jnp.where(emask, eex / sm(eex), -1.0)
    p1 = mx(ep)
    i1 = first(ep == p1)
    ep2 = jnp.where(lane == i1, -1.0, ep)
    p2 = mx(ep2)
    i2 = first(ep2 == p2)
    den = p1 + p2
    return jnp.where(lane == i1, g_prob * p1 / den, 0.0) + jnp.where(lane == i2, g_prob * p2 / den, 0.0)


def _finish_body(x_ref, mix_ref, wo_ref, g_ref, wr1_ref, wr2_ref, wr3_ref, wg_ref, wu_ref, wd_ref,
                 y_ref, h2b, cw, acc, *, exact):
    e = pl.program_id(1)

    @pl.when(e == 0)
    def _():
        if exact:
            proj = jnp.dot(mix_ref[...], wo_ref[...], precision=lax.Precision.HIGHEST, preferred_element_type=F32)
        else:
            proj = jnp.dot(mix_ref[...], wo_ref[...], preferred_element_type=F32)
        y1 = x_ref[...] + proj
        ms = jnp.mean(y1 * y1, axis=-1, keepdims=True)
        h2 = y1 * lax.rsqrt(ms + EPS) * g_ref[...]
        h2b[...] = h2.astype(BF16)
        acc[...] = y1
        a1, a2, a3 = _split3(h2)
        d = lambda a, b_ref: jnp.dot(a, b_ref[...], preferred_element_type=F32)
        logits = (d(a1, wr1_ref) + d(a1, wr2_ref) + d(a2, wr1_ref)
                  + d(a1, wr3_ref) + d(a2, wr2_ref) + d(a3, wr1_ref))
        cw[...] = _route(logits)

    hb = h2b[...]
    cwv = cw[...]
    lane = lax.broadcasted_iota(I32, cwv.shape, 1)
    for j in range(EXPERTS_PER_STEP):
        eid = e * EXPERTS_PER_STEP + j
        gte = jnp.dot(hb, wg_ref[j], preferred_element_type=F32)
        up = jnp.dot(hb, wu_ref[j], preferred_element_type=F32)
        cwe = jnp.sum(jnp.where(lane == N_GROUPS + eid, cwv, 0.0), axis=-1, keepdims=True)
        a = (gte * _sigmoid(gte)) * up * cwe
        acc[...] += jnp.dot(a.astype(BF16), wd_ref[j], preferred_element_type=F32)

    @pl.when(e == pl.num_programs(1) - 1)
    def _():
        y_ref[...] = acc[...]


def _finish(x_all, mix, wo, ffn_g, wr1, wr2, wr3, wg, wu, wd, tm, exact):
    tp = x_all.shape[0]
    eb = EXPERTS_PER_STEP
    row = lambda w: pl.BlockSpec((tm, w), lambda i, e: (i, 0))
    full = lambda a: pl.BlockSpec(a.shape, lambda i, e: (0,) * a.ndim)
    return pl.pallas_call(
        functools.partial(_finish_body, exact=exact),
        grid=(tp // tm, N_EXPERTS // eb),
        in_specs=[row(D_MODEL), row(2 * HW), full(wo), full(ffn_g), full(wr1), full(wr2), full(wr3),
                  pl.BlockSpec((eb, D_MODEL, D_EXPERT), lambda i, e: (e, 0, 0)),
                  pl.BlockSpec((eb, D_MODEL, D_EXPERT), lambda i, e: (e, 0, 0)),
                  pl.BlockSpec((eb, D_EXPERT, D_MODEL), lambda i, e: (e, 0, 0))],
        out_specs=row(D_MODEL),
        out_shape=jax.ShapeDtypeStruct((tp, D_MODEL), F32),
        scratch_shapes=[pltpu.VMEM((tm, D_MODEL), BF16), pltpu.VMEM((tm, LANES), F32),
                        pltpu.VMEM((tm, D_MODEL), F32)],
        compiler_params=_cparams(("parallel", "arbitrary")),
        name="finish_exact" if exact else "finish",
    )(x_all, mix, wo, ffn_g, wr1, wr2, wr3, wg, wu, wd)


def _tile_heads(g):
    return jnp.tile(g.astype(F32), HEADS)[None, :]


def kernel(x_prompt, x_sample, cache_k, cache_v, cache_idx_k, state_gdn, state_conv, page_table, attn_norm_g, w_in,
           conv_w, A_log, dt_bias, gdn_norm_g, q_norm_g, k_norm_g, idx_k_norm_g, w_out, ffn_norm_g, w_router_group,
           w_router_expert, w_gate, w_up, w_down):
    assert w_in.shape[0] == 1 and x_prompt.shape[0] == 1 and x_sample.shape[1] == 1
    t = x_prompt.shape[1]
    nb = x_sample.shape[0]
    assert t % BQ == 0 and t % CHUNK == 0 and BQ == BK and t % TM_FIN == 0

    offs = np.concatenate([[0], np.cumsum(IN_SPLITS)])
    seg = lambda i: w_in[0][:, offs[i]:offs[i + 1]]
    tail_pad = LANES - (DH + 3 * HEADS)
    w_r = jnp.concatenate([seg(0), seg(1), seg(4), seg(5), seg(6), seg(7), seg(8), seg(2), seg(3), seg(9),
                           jnp.zeros((D_MODEL, tail_pad), F32)], axis=1).astype(F32)
    hid = np.arange(HW) // DH
    bd = jnp.asarray(hid[:, None] == hid[None, :], BF16)
    tri = np.arange(CHUNK)
    ltri = jnp.asarray(tri[:, None] >= tri[None, :], BF16)
    utri = jnp.asarray(tri[:, None] <= tri[None, :], BF16)
    eye = jnp.asarray(tri[:, None] == tri[None, :], BF16)

    norm_args = (attn_norm_g.astype(F32), bd, _tile_heads(q_norm_g[0]), _tile_heads(k_norm_g[0]),
                 idx_k_norm_g.astype(F32))
    x_p = x_prompt[0].astype(F32)
    x_s = x_sample[:, 0].astype(F32)
    (qkv, z, q_bf, k_f, k_bf, v_f, v_bf, qi_bf, ki_f, misc) = _inproj(
        x_p, norm_args[0], w_r.astype(BF16), *norm_args[1:], tm=TM_IN, exact=False)
    (qkv_s, z_s, q_s, k_s, _, v_s, _, qi_s, ki_s, misc_s) = _inproj(
        x_s, norm_args[0], w_r, *norm_args[1:], tm=nb, exact=True)
    split_misc = lambda m: (m[:, DH:DH + HEADS], m[:, DH + HEADS:DH + 2 * HEADS],
                            m[:, DH + 2 * HEADS:DH + 3 * HEADS] * ((HEADS * DH) ** -0.5))
    a_p, b_p, wi_p = split_misc(misc)
    a_s, b_s, wi_s = split_misc(misc_s)

    rep = lambda a: jnp.repeat(a, DH, axis=-1)
    alog = A_log[0].astype(F32)
    dtb = dt_bias[0].astype(F32)
    n_chunks = t // CHUNK
    o_g_p, s_p = _gdn_prompt(
        qkv, rep(a_p), rep(b_p), a_p.reshape(n_chunks, CHUNK, HEADS).swapaxes(1, 2), z,
        conv_w[0].astype(F32), rep(alog)[None, :], rep(dtb)[None, :],
        jnp.broadcast_to(alog[:, None], (HEADS, CHUNK)), jnp.broadcast_to(dtb[:, None], (HEADS, CHUNK)),
        _tile_heads(gdn_norm_g[0]), bd, ltri, utri, eye)
    conv_p = qkv[t - (CONV_WIDTH - 1):t]

    ext = jnp.concatenate([state_conv[0].astype(F32), qkv_s[:, None, :]], axis=1)
    ext_t = ext.transpose(1, 2, 0).reshape(CONV_WIDTH, 3, HEADS, DH, nb)
    cw_t = jnp.broadcast_to(conv_w[0].astype(F32).reshape(CONV_WIDTH, 3, HEADS, DH, 1), ext_t.shape)
    row_t = lambda a: jnp.broadcast_to(a, (HEADS, nb)).reshape(HEADS, 1, nb)
    o_g_st, s_st = _gdn_step(
        ext_t[:, 0], ext_t[:, 1], ext_t[:, 2], cw_t[:, 0], cw_t[:, 1], cw_t[:, 2],
        row_t(a_s.T), row_t(b_s.T), row_t(alog[:, None]), row_t(dtb[:, None]),
        z_s.T.reshape(HEADS, DH, nb), jnp.broadcast_to(gdn_norm_g[0].astype(F32)[:, None], (DH, nb)),
        state_gdn[0].astype(F32).transpose(1, 2, 3, 0))
    o_g_s = o_g_st.reshape(HW, nb).T
    s_s = s_st.transpose(3, 0, 1, 2)
    conv_s = ext[:, 1:]

    nkt = t // BK
    o_a_t = _dsa_prompt(
        ki_f.astype(BF16).reshape(nkt, BK, DH), qi_bf.T, wi_p.T, q_bf.T,
        k_bf.reshape(nkt, BK, HW), v_bf.reshape(nkt, BK, HW).swapaxes(1, 2), min(TOPK_MAX, t // 4))
    o_a_p = o_a_t.T

    n_pool, page = cache_idx_k.shape[1], cache_idx_k.shape[2]
    past = page_table.shape[1] * page
    pt = page_table.astype(I32)
    mask, new_sel = _dsa_select(
        pt, cache_idx_k[0].astype(F32).transpose(0, 2, 1), qi_s.reshape(nb, HEADS, DH),
        jnp.broadcast_to(wi_s[:, :, None], (nb, HEADS, page)), ki_s[:, None, :], min(TOPK_MAX, (past + 1) // 4))
    col = lambda a: a.reshape(nb, HEADS, DH, 1)
    o_a_s = _dsa_attend(
        pt, cache_k[0].astype(F32).transpose(0, 2, 3, 1), cache_v[0].astype(F32).transpose(0, 2, 3, 1),
        mask, new_sel, col(q_s), col(k_s), col(v_s)).reshape(nb, HW)

    mix_p = jnp.concatenate([o_g_p, o_a_p.astype(BF16)], axis=1)
    mix_s = jnp.concatenate([o_g_s, o_a_s], axis=1)
    w_router = jnp.concatenate([
        w_router_group[0], w_router_expert[0].transpose(1, 0, 2).reshape(D_MODEL, N_EXPERTS),
        jnp.zeros((D_MODEL, LANES - N_GROUPS - N_EXPERTS), F32)], axis=1).astype(F32)
    wr1 = w_router.astype(BF16)
    wr2 = (w_router - wr1.astype(F32)).astype(BF16)
    wr3 = (w_router - wr1.astype(F32) - wr2.astype(F32)).astype(BF16)
    experts = (w_gate[0].astype(BF16), w_up[0].astype(BF16), w_down[0].astype(BF16))
    y_p = _finish(x_p, mix_p, w_out[0].astype(BF16), ffn_norm_g.astype(F32), wr1, wr2, wr3, *experts,
                  tm=TM_FIN, exact=False)
    y_s = _finish(x_s, mix_s, w_out[0].astype(F32), ffn_norm_g.astype(F32), wr1, wr2, wr3, *experts,
                  tm=nb, exact=True)

    return (y_p[None], y_s[:, None, :],
            k_f.reshape(1, 1, t, HEADS, DH), v_f.reshape(1, 1, t, HEADS, DH), ki_f[None, None],
            s_p[None, None], conv_p[None, None],
            k_s.reshape(1, nb, 1, HEADS, DH), v_s.reshape(1, nb, 1, HEADS, DH),
            ki_s.reshape(1, nb, 1, DH), s_s[None], conv_s[None])
```

```python
import functools

import jax
import jax.numpy as jnp
import numpy as np
from jax import lax
from jax.experimental import pallas as pl
from jax.experimental.pallas import tpu as pltpu

F32 = jnp.float32
BF16 = jnp.bfloat16
I32 = jnp.int32
EPS = 1e-6
NEG = -1e30
INT_MIN = -(2 ** 31)

D_MODEL = 1024
HEADS = 8
DH = 64
HW = HEADS * DH
CONV_DIM = 3 * HW
CONV_WIDTH = 4
CHUNK = 64
TOPK_MAX = 256
N_GROUPS = 4
EPG = 8
N_EXPERTS = N_GROUPS * EPG
D_EXPERT = 256
LANES = 128
IN_SPLITS = (CONV_DIM, HW, HEADS, HEADS, HW, HW, HW, HW, DH, HEADS)

TM_IN = 256
TM_FIN = 512
EXPERTS_PER_STEP = 4
BQ = 256
BK = 256
KV_TILES_PER_BLOCK = 8
V_ROWS = DH + 16
BRACKET_BITS = 13
LOG2E = 1.4426950408889634
PAGES_PER_STEP = 16
VMEM_LIMIT = 52 * 1024 * 1024


def _cparams(sem):
    return pltpu.CompilerParams(dimension_semantics=sem, vmem_limit_bytes=VMEM_LIMIT)


def _sigmoid(x):
    return 1.0 / (1.0 + jnp.exp(-x))


def _softplus(x):
    return jnp.maximum(x, 0.0) + jnp.log(1.0 + jnp.exp(-jnp.abs(x)))


def _mm(a, b):
    return jnp.dot(a.astype(BF16), b.astype(BF16), preferred_element_type=F32)


def _mm_nt(a, b):
    return lax.dot_general(a.astype(BF16), b.astype(BF16), (((1,), (1,)), ((), ())),
                           preferred_element_type=F32)


def _split2(x):
    hi = x.astype(BF16)
    return hi, (x - hi.astype(F32)).astype(BF16)


def _mm3(a, b):
    ah, al = _split2(a)
    bh, bl = _split2(b)
    d = lambda x, y: jnp.dot(x, y, preferred_element_type=F32)
    return d(ah, bh) + d(ah, bl) + d(al, bh)


_mm_inv = _mm3


def _mm_nt3(a, b):
    ah, al = _split2(a)
    bh, bl = _split2(b)
    d = lambda x, y: lax.dot_general(x, y, (((1,), (1,)), ((), ())), preferred_element_type=F32)
    return d(ah, bh) + d(ah, bl) + d(al, bh)


def _split3(x):
    x1 = x.astype(BF16)
    r = x - x1.astype(F32)
    x2 = r.astype(BF16)
    x3 = (r - x2.astype(F32)).astype(BF16)
    return x1, x2, x3


def _dot_ones_l(ones_bf, x):
    x1, x2, x3 = _split3(x)
    d = lambda p: jnp.dot(ones_bf, p, preferred_element_type=F32)
    return d(x1) + d(x2) + d(x3)


def _dot_ones_r(x, ones_bf):
    x1, x2, x3 = _split3(x)
    d = lambda p: jnp.dot(p, ones_bf, preferred_element_type=F32)
    return d(x1) + d(x2) + d(x3)


def _head_sum(y, bd):
    return _dot_ones_r(y, bd)


def _inproj_body(x_ref, g_ref, w_ref, bd_ref, qg_ref, kg_ref, ikg_ref,
                 qkv_ref, z_ref, q_ref, k_ref, kb_ref, v_ref, vb_ref, qi_ref, ki_ref, misc_ref, *, exact):
    x = x_ref[...]
    ms = jnp.mean(x * x, axis=-1, keepdims=True)
    h = x * lax.rsqrt(ms + EPS) * g_ref[...]
    if not exact:
        h = h.astype(BF16)

    def seg(lo, n):
        if exact:
            return jnp.dot(h, w_ref[:, lo:lo + n], precision=lax.Precision.HIGHEST, preferred_element_type=F32)
        return jnp.dot(h, w_ref[:, lo:lo + n], preferred_element_type=F32)

    for j in range(3):
        qkv_ref[:, j * HW:(j + 1) * HW] = seg(j * HW, HW)
    z_ref[...] = seg(3 * HW, HW)
    bd = bd_ref[...]

    def head_rms(y, gain):
        return y * lax.rsqrt(_head_sum(y * y, bd) * (1.0 / DH) + EPS) * gain

    q = head_rms(seg(4 * HW, HW), qg_ref[...])
    q_ref[...] = (q * (DH ** -0.5 * (1.0 if exact else LOG2E))).astype(q_ref.dtype)
    k = head_rms(seg(5 * HW, HW), kg_ref[...])
    k_ref[...] = k
    kb_ref[...] = k.astype(BF16)
    v = seg(6 * HW, HW)
    v_ref[...] = v
    vb_ref[...] = v.astype(BF16)
    qi_ref[...] = seg(7 * HW, HW).astype(qi_ref.dtype)
    tail = seg(8 * HW, LANES)
    ki = tail[:, :DH]
    ki_ms = jnp.mean(ki * ki, axis=-1, keepdims=True)
    ki_ref[...] = ki * lax.rsqrt(ki_ms + EPS) * ikg_ref[...]
    misc_ref[...] = tail


def _inproj(x_all, attn_g, w_r, bd, qg, kg, ikg, tm, exact):
    tp = x_all.shape[0]
    nw = w_r.shape[1]
    qdt = F32 if exact else BF16
    row = lambda w: pl.BlockSpec((tm, w), lambda i: (i, 0))
    full = lambda a: pl.BlockSpec(a.shape, lambda i: (0,) * a.ndim)
    out_shapes = [
        jax.ShapeDtypeStruct((tp, CONV_DIM), F32),
        jax.ShapeDtypeStruct((tp, HW), F32),
        jax.ShapeDtypeStruct((tp, HW), qdt),
        jax.ShapeDtypeStruct((tp, HW), F32),
        jax.ShapeDtypeStruct((tp, HW), BF16),
        jax.ShapeDtypeStruct((tp, HW), F32),
        jax.ShapeDtypeStruct((tp, HW), BF16),
        jax.ShapeDtypeStruct((tp, HW), qdt),
        jax.ShapeDtypeStruct((tp, DH), F32),
        jax.ShapeDtypeStruct((tp, LANES), F32),
    ]
    out_specs = [row(CONV_DIM), row(HW), row(HW), row(HW), row(HW), row(HW), row(HW), row(HW),
                 row(DH), row(LANES)]
    return pl.pallas_call(
        functools.partial(_inproj_body, exact=exact),
        grid=(tp // tm,),
        in_specs=[row(D_MODEL), full(attn_g), pl.BlockSpec((D_MODEL, nw), lambda i: (0, 0)), full(bd),
                  full(qg), full(kg), full(ikg)],
        out_specs=out_specs,
        out_shape=out_shapes,
        compiler_params=_cparams(("parallel",)),
        name="inproj_exact" if exact else "inproj",
    )(x_all, attn_g, w_r, bd, qg, kg, ikg)


def _gdn_body(qkv_ref, ae_ref, be_ref, at_ref, z_ref, cw_ref, alog_ref, dtb_ref, alogt_ref, dtbt_ref,
              ng_ref, bd_ref, ltri_ref, utri_ref, eye_ref,
              o_ref, s_out_ref, xb, s_scr):
    c = pl.program_id(0)

    @pl.when(c == 0)
    def _():
        xb[0:8, :] = jnp.zeros((8, CONV_DIM), F32)
        s_scr[...] = jnp.zeros_like(s_scr)

    xb[8:8 + CHUNK, :] = qkv_ref[...]
    base = 8 - (CONV_WIDTH - 1)
    conv = xb[base:base + CHUNK, :] * cw_ref[0:1, :]
    for i in range(1, CONV_WIDTH):
        conv = conv + xb[base + i:base + i + CHUNK, :] * cw_ref[i:i + 1, :]
    xb[0:8, :] = xb[CHUNK:CHUNK + 8, :]
    act = conv * _sigmoid(conv)
    q = act[:, 0:HW]
    k = act[:, HW:2 * HW]
    v = act[:, 2 * HW:3 * HW]
    bd = bd_ref[...]
    qn = q * lax.rsqrt(_head_sum(q * q, bd) + EPS) * (DH ** -0.5)
    kn = k * lax.rsqrt(_head_sum(k * k, bd) + EPS)
    beta = _sigmoid(be_ref[...])
    g = -jnp.exp(alog_ref[...]) * _softplus(ae_ref[...] + dtb_ref[...])
    gc = _dot_ones_l(ltri_ref[...], g)
    gt = -jnp.exp(alogt_ref[...]) * _softplus(at_ref[0] + dtbt_ref[...])
    gct = _dot_ones_r(gt, utri_ref[...])
    gl = gc[CHUNK - 1:CHUNK, :]
    eg = jnp.exp(gc)
    kb = kn * beta
    vb = v * beta
    kbe = kb * eg
    qe = qn * eg
    kdec = kn * jnp.exp(gl - gc)
    egl = jnp.exp(gl)
    ri = lax.broadcasted_iota(I32, (CHUNK, CHUNK), 0)
    ci = lax.broadcasted_iota(I32, (CHUNK, CHUNK), 1)
    causal = ri >= ci
    strict = ri > ci
    eye = eye_ref[...]
    heads = range(HEADS)
    hsl = [slice(DH * h, DH * (h + 1)) for h in heads]
    dec = [jnp.exp(jnp.where(causal, gc[:, hsl[h]] - gct[h:h + 1, :], NEG)) for h in heads]
    kh = [kn[:, hsl[h]].astype(BF16) for h in heads]
    p = [-jnp.where(strict, _mm_nt(kb[:, hsl[h]], kh[h]) * dec[h], 0.0) for h in heads]
    attn = [_mm_nt(qn[:, hsl[h]], kh[h]) * dec[h] for h in heads]
    kdt = [_mm_nt(eye, kdec[:, hsl[h]]) for h in heads]
    xx = [jnp.concatenate([vb[:, hsl[h]], kbe[:, hsl[h]]], axis=1) for h in heads]
    for r in range(6):
        xx = [xx[h] + _mm_inv(p[h], xx[h]) for h in heads]
        if r < 5:
            p = [_mm_inv(p[h], p[h]) for h in heads]
    sh = [s_scr[h] for h in heads]
    v_new = [xx[h][:, :DH] - _mm(xx[h][:, DH:], sh[h]) for h in heads]
    outs = [_mm(qe[:, hsl[h]], sh[h]) + _mm(attn[h], v_new[h]) for h in heads]
    for h in heads:
        s_scr[h] = sh[h] * egl[:, hsl[h]] + _mm(kdt[h], v_new[h])
    o = jnp.concatenate(outs, axis=1)
    on = o * lax.rsqrt(_head_sum(o * o, bd) * (1.0 / DH) + EPS) * ng_ref[...]
    z = z_ref[...]
    o_ref[...] = (on * (z * _sigmoid(z))).astype(BF16)

    @pl.when(c == pl.num_programs(0) - 1)
    def _():
        s_out_ref[...] = s_scr[...]


def _gdn_prompt(qkv, a_e, b_e, a_t, z, conv_w, alog_e, dtb_e, alog_t, dtb_t, ng, bd, ltri, utri, eye):
    t = qkv.shape[0]
    row = lambda w: pl.BlockSpec((CHUNK, w), lambda i: (i, 0))
    full = lambda a: pl.BlockSpec(a.shape, lambda i: (0,) * a.ndim)
    return pl.pallas_call(
        _gdn_body,
        grid=(t // CHUNK,),
        in_specs=[row(CONV_DIM), row(HW), row(HW), pl.BlockSpec((1, HEADS, CHUNK), lambda i: (i, 0, 0)), row(HW),
                  full(conv_w), full(alog_e), full(dtb_e), full(alog_t), full(dtb_t), full(ng), full(bd),
                  full(ltri), full(utri), full(eye)],
        out_specs=[row(HW), pl.BlockSpec((HEADS, DH, DH), lambda i: (0, 0, 0))],
        out_shape=[jax.ShapeDtypeStruct((t, HW), BF16), jax.ShapeDtypeStruct((HEADS, DH, DH), F32)],
        scratch_shapes=[pltpu.VMEM((CHUNK + 8, CONV_DIM), F32), pltpu.VMEM((HEADS, DH, DH), F32)],
        compiler_params=_cparams(("arbitrary",)),
        name="gdn_prompt",
    )(qkv, a_e, b_e, a_t, z, conv_w, alog_e, dtb_e, alog_t, dtb_t, ng, bd, ltri, utri, eye)


def _gdn_step_body(xq_ref, xk_ref, xv_ref, cq_ref, ck_ref, cv_ref, a_ref, b_ref, alog_ref, dtb_ref, z_ref, ng_ref,
                   s0_ref, o_ref, s1_ref):
    def conv(x_ref, c_ref):
        acc = x_ref[0, 0] * c_ref[0, 0]
        for i in range(1, CONV_WIDTH):
            acc = acc + x_ref[i, 0] * c_ref[i, 0]
        return acc * _sigmoid(acc)

    q = conv(xq_ref, cq_ref)
    k = conv(xk_ref, ck_ref)
    v = conv(xv_ref, cv_ref)
    qn = q * lax.rsqrt(jnp.sum(q * q, axis=0, keepdims=True) + EPS) * (DH ** -0.5)
    kn = k * lax.rsqrt(jnp.sum(k * k, axis=0, keepdims=True) + EPS)
    beta = _sigmoid(b_ref[0])
    eg = jnp.exp(-jnp.exp(alog_ref[0]) * _softplus(a_ref[0] + dtb_ref[0]))
    ks = jnp.zeros_like(v)
    for d in range(DH):
        ks = ks + kn[d:d + 1, :] * s0_ref[0, d]
    delta = (v - ks * eg) * beta
    o = jnp.zeros_like(v)
    for d in range(DH):
        s_new = s0_ref[0, d] * eg + kn[d:d + 1, :] * delta
        s1_ref[0, d] = s_new
        o = o + qn[d:d + 1, :] * s_new
    on = o * lax.rsqrt(jnp.mean(o * o, axis=0, keepdims=True) + EPS) * ng_ref[...]
    z = z_ref[0]
    o_ref[0] = on * (z * _sigmoid(z))


def _gdn_step(xq, xk, xv, cq, ck, cv, a_t, b_t, alog_t, dtb_t, z_t, ng_t, s0_t):
    b = xq.shape[-1]
    x_spec = pl.BlockSpec((CONV_WIDTH, 1, DH, b), lambda h: (0, h, 0, 0))
    r_spec = pl.BlockSpec((1, 1, b), lambda h: (h, 0, 0))
    s_spec = pl.BlockSpec((1, DH, DH, b), lambda h: (h, 0, 0, 0))
    hd_spec = pl.BlockSpec((1, DH, b), lambda h: (h, 0, 0))
    return pl.pallas_call(
        _gdn_step_body,
        grid=(HEADS,),
        in_specs=[x_spec, x_spec, x_spec, x_spec, x_spec, x_spec, r_spec, r_spec, r_spec, r_spec, hd_spec,
                  pl.BlockSpec((DH, b), lambda h: (0, 0)), s_spec],
        out_specs=[hd_spec, s_spec],
        out_shape=[jax.ShapeDtypeStruct((HEADS, DH, b), F32), jax.ShapeDtypeStruct((HEADS, DH, DH, b), F32)],
        compiler_params=_cparams(("parallel",)),
        name="gdn_step",
    )(xq, xk, xv, cq, ck, cv, a_t, b_t, alog_t, dtb_t, z_t, ng_t, s0_t)


def _count_tiles(sc, n_tiles, pred):
    def body(kt, cnt):
        hit = jnp.where(pred(kt, sc[kt]), 1, 0).astype(I32)
        return cnt + jnp.sum(hit.reshape(BK // 8, 8, BQ), axis=0)

    cnt = lax.fori_loop(0, n_tiles, body, jnp.zeros((8, BQ), I32))
    return jnp.sum(cnt, axis=0, keepdims=True)


def _count_ge(sc, n_tiles, cand):
    return _count_tiles(sc, n_tiles, lambda kt, s: s >= cand)


def _key_to_f32(key):
    f = pltpu.bitcast(jnp.where(key < 0, key ^ jnp.int32(0x7FFFFFFF), key), F32)
    return jnp.where(key <= INT_MIN + 0x7FFFFF, -jnp.inf, f)


def _greedy_key(count_fn, k, shape, bits):
    lo = jnp.where(count_fn(jnp.zeros(shape, F32)) >= k, 0, INT_MIN).astype(I32)

    def bit(i, lo):
        cand = lo + jnp.left_shift(jnp.int32(1), 30 - i)
        return jnp.where(count_fn(_key_to_f32(cand)) >= k, cand, lo)

    return lax.fori_loop(0, bits, bit, lo)


def _kth_largest(count_fn, k, shape):
    return _key_to_f32(_greedy_key(count_fn, k, shape, 31))


def _kth_largest_bracketed(count_fn, k, lo, hi):
    def unresolved(lo, hi, c_lo):
        gap = hi - lo
        return ((gap > 1) | (gap < 0)) & (c_lo != k)

    def cond(st):
        i, lo, hi, c_lo = st
        return (i < 34) & (jnp.max(jnp.where(unresolved(lo, hi, c_lo), 1, 0)) > 0)

    def body(st):
        i, lo, hi, c_lo = st
        mid = lo + lax.shift_right_logical(hi - lo, 1)
        c = count_fn(_key_to_f32(mid))
        take = c >= k
        return i + 1, jnp.where(take, mid, lo), jnp.where(take, hi, mid), jnp.where(take, c, c_lo)

    _, lo, _, c_lo = lax.while_loop(cond, body, (jnp.int32(0), lo, hi, count_fn(_key_to_f32(lo))))
    return lo, c_lo


def _dsa_body(kib_ref, qit_ref, wt_ref, qt_ref, k_ref, vt_ref, o_ref, sc, gm, thr, m_s, a_s, lg_s, acc, qz, *,
              topk, idx_bits):
    qb = pl.program_id(0)
    kb = pl.program_id(1)
    q0 = qb * BQ
    n_tiles = (q0 + BQ + BK - 1) // BK
    qpos = q0 + lax.broadcasted_iota(I32, (BK, BQ), 1)
    rowi = lax.broadcasted_iota(I32, (BK, BQ), 0)

    @pl.when(kb == 0)
    def _score():
        w = wt_ref[...]

        def tile(kt, carry):
            ki_t = kib_ref[kt]
            s = jnp.zeros((BK, BQ), F32)
            for h in range(HEADS):
                sh = jnp.dot(ki_t, qit_ref[DH * h:DH * (h + 1), :], preferred_element_type=F32)
                s = s + jnp.maximum(sh, 0.0) * w[h:h + 1, :]
            vis = (kt * BK + rowi) <= qpos
            s = jnp.where(vis, s, -jnp.inf)
            sc[kt] = s
            gm[pl.ds(pl.multiple_of(kt * 8, 8), 8), :] = jnp.max(s.reshape(BK // 8, 8, BQ), axis=0)
            return carry

        gm[...] = jnp.full_like(gm, -jnp.inf)
        lax.fori_loop(0, n_tiles, tile, 0)

        def gm_count(cand):
            hit = jnp.where(gm[...] >= cand, 1, 0).astype(I32)
            return jnp.sum(jnp.sum(hit.reshape(gm.shape[0] // 8, 8, BQ), axis=0), axis=0, keepdims=True)

        lo = _greedy_key(gm_count, topk, (1, BQ), BRACKET_BITS)
        hi = _greedy_key(gm_count, 1, (1, BQ), BRACKET_BITS) + (1 << (31 - BRACKET_BITS))
        key, c_key = _kth_largest_bracketed(functools.partial(_count_ge, sc, n_tiles), topk, lo, hi)
        t = _key_to_f32(key)
        thr[...] = t
        over = (c_key > topk) & (t > -jnp.inf)

        @pl.when(jnp.max(jnp.where(over, 1, 0)) > 0)
        def _break_ties():
            room = topk - _count_tiles(sc, n_tiles, lambda kt, s: s > t)

            def bit(i, x):
                cand = x + jnp.left_shift(jnp.int32(1), idx_bits - 1 - i)
                below = _count_tiles(sc, n_tiles, lambda kt, s: (s == t) & ((kt * BK + rowi) < cand))
                return jnp.where(below < room, cand, x)

            last_kept = lax.fori_loop(0, idx_bits, bit, jnp.zeros((1, BQ), I32))

            def drop(kt, carry):
                s = sc[kt]
                sc[kt] = jnp.where((s == t) & ((kt * BK + rowi) > last_kept), -jnp.inf, s)
                return carry

            lax.fori_loop(0, n_tiles, drop, 0)

        m_s[...] = jnp.full_like(m_s, NEG)
        acc[...] = jnp.zeros_like(acc)
        zero = jnp.zeros((DH, BQ), BF16)
        for h in range(HEADS):
            qh = qt_ref[DH * h:DH * (h + 1), :]
            qz[h, 0:DH, :] = qh if h % 2 == 0 else zero
            qz[h, DH:2 * DH, :] = zero if h % 2 == 0 else qh

    @pl.when(kb * (KV_TILES_PER_BLOCK * BK) < q0 + BQ)
    def _attend():
        t0 = kb * KV_TILES_PER_BLOCK
        th = thr[...]

        def tile(j, carry):
            gt = t0 + j
            msk = (sc[gt] >= th) & ((gt * BK + rowi) <= qpos)
            bias = jnp.where(msk, 0.0, NEG)
            for h in range(HEADS):
                pair = LANES * (h // 2)
                lg = jnp.dot(k_ref[j, :, pair:pair + LANES], qz[h], preferred_element_type=F32) + bias
                lg_s[h] = lg
                m_old = m_s[h]
                m_new = jnp.maximum(m_old, jnp.max(lg, axis=0, keepdims=True))
                a_s[h] = jnp.exp2(m_old - m_new)
                m_s[h] = m_new
            for h in range(HEADS):
                p = jnp.exp2(lg_s[h] - m_s[h])
                pv = jnp.dot(vt_ref[j, h], p.astype(BF16), preferred_element_type=F32)
                acc[h] = a_s[h] * acc[h] + pv
            return carry

        lax.fori_loop(0, jnp.minimum(KV_TILES_PER_BLOCK, n_tiles - t0), tile, 0)

    @pl.when(kb == pl.num_programs(1) - 1)
    def _finish():
        for h in range(HEADS):
            o_ref[DH * h:DH * (h + 1), :] = acc[h, 0:DH, :] / acc[h, DH:DH + 1, :]


def _dsa_prompt(kib3, qit, wt, qt, k3, vt3, topk):
    nkt = kib3.shape[0]
    t = nkt * BK
    nq = t // BQ
    nkb = -(-nkt // KV_TILES_PER_BLOCK)
    kvb = min(KV_TILES_PER_BLOCK, nkt)
    span = kvb * BK

    def kv_idx(qb, kb):
        return (jnp.minimum(kb, (qb * BQ + BQ - 1) // span), 0, 0)

    return pl.pallas_call(
        functools.partial(_dsa_body, topk=topk, idx_bits=(t - 1).bit_length()),
        grid=(nq, nkb),
        in_specs=[pl.BlockSpec((nkt, BK, DH), lambda qb, kb: (0, 0, 0)),
                  pl.BlockSpec((HW, BQ), lambda qb, kb: (0, qb)),
                  pl.BlockSpec((HEADS, BQ), lambda qb, kb: (0, qb)),
                  pl.BlockSpec((HW, BQ), lambda qb, kb: (0, qb)),
                  pl.BlockSpec((kvb, BK, HW), kv_idx),
                  pl.BlockSpec((kvb, HEADS, V_ROWS, BK), lambda qb, kb: kv_idx(qb, kb) + (0,))],
        out_specs=pl.BlockSpec((HW, BQ), lambda qb, kb: (0, qb)),
        out_shape=jax.ShapeDtypeStruct((HW, t), F32),
        scratch_shapes=[pltpu.VMEM((nkt, BK, BQ), F32), pltpu.VMEM((nkt * 8, BQ), F32), pltpu.VMEM((1, BQ), F32),
                        pltpu.VMEM((HEADS, 1, BQ), F32), pltpu.VMEM((HEADS, 1, BQ), F32),
                        pltpu.VMEM((HEADS, BK, BQ), F32), pltpu.VMEM((HEADS, V_ROWS, BQ), F32),
                        pltpu.VMEM((HEADS, 2 * DH, BQ), BF16)],
        compiler_params=_cparams(("parallel", "arbitrary")),
        name="dsa_prompt",
    )(kib3, qit, wt, qt, k3, vt3)


def _prefix_count(mf, upper, lower_strict):
    within = jnp.dot(mf.astype(BF16), upper, preferred_element_type=F32)
    tot = jnp.broadcast_to(within[:, -1:], within.shape)
    return within + jnp.dot(lower_strict, tot.astype(BF16), preferred_element_type=F32)


def _dsa_select_body(pt_ref, *refs, n_chunks, page, topk):
    pp = PAGES_PER_STEP
    ik_refs = refs[0:pp]
    qi_ref, w_ref, kin_ref, mask_ref, meta_ref, sc = refs[pp:]
    del pt_ref
    c = pl.program_id(1)
    n_pages = n_chunks * pp
    qi = qi_ref[...]
    w = w_ref[...]
    srows = []
    for i in range(pp):
        s8 = _mm3(qi, ik_refs[i][...])
        srows.append(jnp.sum(jnp.maximum(s8, 0.0) * w, axis=0, keepdims=True))
    sc[c] = jnp.concatenate(srows, axis=0)

    @pl.when(c == n_chunks - 1)
    def _select():
        s_new = jnp.sum(qi * kin_ref[...], axis=1, keepdims=True)
        s_new = jnp.sum(jnp.maximum(s_new, 0.0) * w[:, 0:1], axis=0, keepdims=True)
        total = lambda m: jnp.sum(jnp.sum(m, axis=1, keepdims=True), axis=0, keepdims=True)
        s_all = sc[...].reshape(n_pages, page)

        def count(cand):
            return total(jnp.where(s_all >= cand, 1.0, 0.0)) + jnp.where(s_new >= cand, 1.0, 0.0)

        thr = _kth_largest(count, topk, (1, 1))
        ri = lax.broadcasted_iota(I32, (page, page), 0)
        ci = lax.broadcasted_iota(I32, (page, page), 1)
        upper = jnp.where(ri <= ci, 1.0, 0.0).astype(BF16)
        rp = lax.broadcasted_iota(I32, (n_pages, n_pages), 0)
        cp = lax.broadcasted_iota(I32, (n_pages, n_pages), 1)
        lower_strict = jnp.where(cp < rp, 1.0, 0.0).astype(BF16)
        gt = s_all > thr
        eq = s_all == thr
        eqf = jnp.where(eq, 1.0, 0.0)
        room = topk - total(jnp.where(gt, 1.0, 0.0)) - jnp.where(s_new > thr, 1.0, 0.0)
        sel = gt | (eq & (_prefix_count(eqf, upper, lower_strict) <= room))
        new_sel = (s_new > thr) | ((s_new == thr) & (total(eqf) < room))
        mask_ref[...] = jnp.where(sel, 1.0, 0.0)
        meta_ref[...] = jnp.where(new_sel, 1.0, 0.0) * jnp.ones(meta_ref.shape, F32)


def _dsa_select(page_table, cik_t, qi_s, w_s, ki_new, topk):
    b, n_pages = page_table.shape
    page = cik_t.shape[2]
    pp = PAGES_PER_STEP
    n_chunks = n_pages // pp

    def ik_spec(i):
        return pl.BlockSpec((None, DH, page), lambda bb, c, pt: (pt[bb, c * pp + i], 0, 0))

    per_b = lambda r, w: pl.BlockSpec((None, r, w), lambda bb, c, pt: (bb, 0, 0))
    grid_spec = pltpu.PrefetchScalarGridSpec(
        num_scalar_prefetch=1,
        grid=(b, n_chunks),
        in_specs=[ik_spec(i) for i in range(pp)] + [per_b(HEADS, DH), per_b(HEADS, page), per_b(1, DH)],
        out_specs=[per_b(n_pages, page), per_b(8, page)],
        scratch_shapes=[pltpu.VMEM((n_chunks, pp, page), F32)],
    )
    return pl.pallas_call(
        functools.partial(_dsa_select_body, n_chunks=n_chunks, page=page, topk=topk),
        grid_spec=grid_spec,
        out_shape=[jax.ShapeDtypeStruct((b, n_pages, page), F32), jax.ShapeDtypeStruct((b, 8, page), F32)],
        compiler_params=_cparams(("parallel", "arbitrary")),
        name="dsa_select",
    )(page_table, *([cik_t] * pp), qi_s, w_s, ki_new)


def _dsa_attend_body(pt_ref, *refs, n_chunks):
    del pt_ref
    pp = PAGES_PER_STEP
    k_refs = refs[0:pp]
    v_refs = refs[pp:2 * pp]
    mask_ref, new_ref, q_ref, kn_ref, vn_ref, o_ref, qb, m_s, l_s, acc = refs[2 * pp:]
    c = pl.program_id(1)

    @pl.when(c == 0)
    def _init():
        qb[...] = jnp.broadcast_to(q_ref[...], qb.shape)
        m_s[...] = jnp.full_like(m_s, NEG)
        l_s[...] = jnp.zeros_like(l_s)
        acc[...] = jnp.zeros_like(acc)

    for i in range(pp):
        sel_row = mask_ref[i:i + 1, :] > 0.0
        for h in range(HEADS):
            lg = jnp.sum(k_refs[i][h] * qb[h], axis=0, keepdims=True)
            lg = jnp.where(sel_row, lg, NEG)
            m_old = m_s[h]
            m_new = jnp.maximum(m_old, jnp.max(lg, axis=1, keepdims=True))
            alpha = jnp.exp(m_old - m_new)
            p = jnp.exp(lg - m_new)
            l_s[h] = alpha * l_s[h] + p
            acc[h] = alpha * acc[h] + p * v_refs[i][h]
            m_s[h] = m_new

    @pl.when(c == n_chunks - 1)
    def _finish():
        new_sel = new_ref[0:1, 0:1].reshape(1, 1, 1) > 0.0
        lg_new = jnp.where(new_sel, jnp.sum(q_ref[...] * kn_ref[...], axis=1, keepdims=True), NEG)
        m_old = m_s[...][:, :, 0:1]
        m_fin = jnp.maximum(m_old, lg_new)
        scale = jnp.exp(m_old - m_fin)
        p_new = jnp.exp(lg_new - m_fin)
        den = jnp.sum(l_s[...], axis=2, keepdims=True) * scale + p_new
        o_ref[...] = (jnp.sum(acc[...], axis=2, keepdims=True) * scale + p_new * vn_ref[...]) / den


def _dsa_attend(page_table, ck_t, cv_t, mask, new_sel, q_c, k_c, v_c):
    b, n_pages = page_table.shape
    page = ck_t.shape[3]
    pp = PAGES_PER_STEP
    n_chunks = n_pages // pp

    def kv_spec(i):
        return pl.BlockSpec((None, HEADS, DH, page), lambda bb, c, pt: (pt[bb, c * pp + i], 0, 0, 0))

    col = pl.BlockSpec((None, HEADS, DH, 1), lambda bb, c, pt: (bb, 0, 0, 0))
    grid_spec = pltpu.PrefetchScalarGridSpec(
        num_scalar_prefetch=1,
        grid=(b, n_chunks),
        in_specs=([kv_spec(i) for i in range(pp)] + [kv_spec(i) for i in range(pp)]
                  + [pl.BlockSpec((None, pp, page), lambda bb, c, pt: (bb, c, 0)),
                     pl.BlockSpec((None, 8, page), lambda bb, c, pt: (bb, 0, 0)), col, col, col]),
        out_specs=col,
        scratch_shapes=[pltpu.VMEM((HEADS, DH, page), F32), pltpu.VMEM((HEADS, 1, page), F32),
                        pltpu.VMEM((HEADS, 1, page), F32), pltpu.VMEM((HEADS, DH, page), F32)],
    )
    return pl.pallas_call(
        functools.partial(_dsa_attend_body, n_chunks=n_chunks),
        grid_spec=grid_spec,
        out_shape=jax.ShapeDtypeStruct((b, HEADS, DH, 1), F32),
        compiler_params=_cparams(("parallel", "arbitrary")),
        name="dsa_attend",
    )(page_table, *([ck_t] * pp), *([cv_t] * pp), mask, new_sel, q_c, k_c, v_c)


def _route(logits):
    lane = lax.broadcasted_iota(I32, logits.shape, 1)
    mx = lambda a: jnp.max(a, axis=-1, keepdims=True)
    sm = lambda a: jnp.sum(a, axis=-1, keepdims=True)
    first = lambda hit: jnp.min(jnp.where(hit, lane, LANES), axis=-1, keepdims=True)
    gmask = lane < N_GROUPS
    lg = jnp.where(gmask, logits, NEG)
    gex = jnp.where(gmask, jnp.exp(lg - mx(lg)), 0.0)
    gp = gex / sm(gex)
    g_prob = mx(gp)
    g_idx = first(gmask & (gp == g_prob))
    e_lo = N_GROUPS + EPG * g_idx
    emask = (lane >= e_lo) & (lane < e_lo + EPG)
    le = jnp.where(emask, logits, NEG)
    eex = jnp.where(emask, jnp.exp(le - mx(le)), 0.0)
    ep = jnp.where(emask, eex / sm(eex), -1.0)
    p1 = mx(ep)
    i1 = first(ep == p1)
    ep2 = jnp.where(lane == i1, -1.0, ep)
    p2 = mx(ep2)
    i2 = first(ep2 == p2)
    den = p1 + p2
    return jnp.where(lane == i1, g_prob * p1 / den, 0.0) + jnp.where(lane == i2, g_prob * p2 / den, 0.0)


def _finish_body(x_ref, mix_ref, wo_ref, g_ref, wr1_ref, wr2_ref, wr3_ref, wg_ref, wu_ref, wd_ref,
                 y_ref, h2b, cw, acc, *, exact):
    e = pl.program_id(1)

    @pl.when(e == 0)
    def _():
        if exact:
            proj = jnp.dot(mix_ref[...], wo_ref[...], precision=lax.Precision.HIGHEST, preferred_element_type=F32)
        else:
            proj = jnp.dot(mix_ref[...], wo_ref[...], preferred_element_type=F32)
        y1 = x_ref[...] + proj
        ms = jnp.mean(y1 * y1, axis=-1, keepdims=True)
        h2 = y1 * lax.rsqrt(ms + EPS) * g_ref[...]
        h2b[...] = h2.astype(BF16)
        acc[...] = y1
        a1, a2, a3 = _split3(h2)
        d = lambda a, b_ref: jnp.dot(a, b_ref[...], preferred_element_type=F32)
        logits = (d(a1, wr1_ref) + d(a1, wr2_ref) + d(a2, wr1_ref)
                  + d(a1, wr3_ref) + d(a2, wr2_ref) + d(a3, wr1_ref))
        cw[...] = _route(logits)

    hb = h2b[...]
    cwv = cw[...]
    lane = lax.broadcasted_iota(I32, cwv.shape, 1)
    for j in range(EXPERTS_PER_STEP):
        eid = e * EXPERTS_PER_STEP + j
        gte = jnp.dot(hb, wg_ref[j], preferred_element_type=F32)
        up = jnp.dot(hb, wu_ref[j], preferred_element_type=F32)
        cwe = jnp.sum(jnp.where(lane == N_GROUPS + eid, cwv, 0.0), axis=-1, keepdims=True)
        a = (gte * _sigmoid(gte)) * up * cwe
        acc[...] += jnp.dot(a.astype(BF16), wd_ref[j], preferred_element_type=F32)

    @pl.when(e == pl.num_programs(1) - 1)
    def _():
        y_ref[...] = acc[...]


def _finish(x_all, mix, wo, ffn_g, wr1, wr2, wr3, wg, wu, wd, tm, exact):
    tp = x_all.shape[0]
    eb = EXPERTS_PER_STEP
    row = lambda w: pl.BlockSpec((tm, w), lambda i, e: (i, 0))
    full = lambda a: pl.BlockSpec(a.shape, lambda i, e: (0,) * a.ndim)
    return pl.pallas_call(
        functools.partial(_finish_body, exact=exact),
        grid=(tp // tm, N_EXPERTS // eb),
        in_specs=[row(D_MODEL), row(2 * HW), full(wo), full(ffn_g), full(wr1), full(wr2), full(wr3),
                  pl.BlockSpec((eb, D_MODEL, D_EXPERT), lambda i, e: (e, 0, 0)),
                  pl.BlockSpec((eb, D_MODEL, D_EXPERT), lambda i, e: (e, 0, 0)),
                  pl.BlockSpec((eb, D_EXPERT, D_MODEL), lambda i, e: (e, 0, 0))],
        out_specs=row(D_MODEL),
        out_shape=jax.ShapeDtypeStruct((tp, D_MODEL), F32),
        scratch_shapes=[pltpu.VMEM((tm, D_MODEL), BF16), pltpu.VMEM((tm, LANES), F32),
                        pltpu.VMEM((tm, D_MODEL), F32)],
        compiler_params=_cparams(("parallel", "arbitrary")),
        name="finish_exact" if exact else "finish",
    )(x_all, mix, wo, ffn_g, wr1, wr2, wr3, wg, wu, wd)


def _tile_heads(g):
    return jnp.tile(g.astype(F32), HEADS)[None, :]


def kernel(x_prompt, x_sample, cache_k, cache_v, cache_idx_k, state_gdn, state_conv, page_table, attn_norm_g, w_in,
           conv_w, A_log, dt_bias, gdn_norm_g, q_norm_g, k_norm_g, idx_k_norm_g, w_out, ffn_norm_g, w_router_group,
           w_router_expert, w_gate, w_up, w_down):
    assert w_in.shape[0] == 1 and x_prompt.shape[0] == 1 and x_sample.shape[1] == 1
    t = x_prompt.shape[1]
    nb = x_sample.shape[0]
    assert t % BQ == 0 and t % CHUNK == 0 and BQ == BK and t % TM_FIN == 0

    offs = np.concatenate([[0], np.cumsum(IN_SPLITS)])
    seg = lambda i: w_in[0][:, offs[i]:offs[i + 1]]
    tail_pad = LANES - (DH + 3 * HEADS)
    w_r = jnp.concatenate([seg(0), seg(1), seg(4), seg(5), seg(6), seg(7), seg(8), seg(2), seg(3), seg(9),
                           jnp.zeros((D_MODEL, tail_pad), F32)], axis=1).astype(F32)
    hid = np.arange(HW) // DH
    bd = jnp.asarray(hid[:, None] == hid[None, :], BF16)
    tri = np.arange(CHUNK)
    ltri = jnp.asarray(tri[:, None] >= tri[None, :], BF16)
    utri = jnp.asarray(tri[:, None] <= tri[None, :], BF16)
    eye = jnp.asarray(tri[:, None] == tri[None, :], BF16)

    norm_args = (attn_norm_g.astype(F32), bd, _tile_heads(q_norm_g[0]), _tile_heads(k_norm_g[0]),
                 idx_k_norm_g.astype(F32))
    x_p = x_prompt[0].astype(F32)
    x_s = x_sample[:, 0].astype(F32)
    (qkv, z, q_bf, k_f, k_bf, v_f, v_bf, qi_bf, ki_f, misc) = _inproj(
        x_p, norm_args[0], w_r.astype(BF16), *norm_args[1:], tm=TM_IN, exact=False)
    (qkv_s, z_s, q_s, k_s, _, v_s, _, qi_s, ki_s, misc_s) = _inproj(
        x_s, norm_args[0], w_r, *norm_args[1:], tm=nb, exact=True)
    split_misc = lambda m: (m[:, DH:DH + HEADS], m[:, DH + HEADS:DH + 2 * HEADS],
                            m[:, DH + 2 * HEADS:DH + 3 * HEADS] * ((HEADS * DH) ** -0.5))
    a_p, b_p, wi_p = split_misc(misc)
    a_s, b_s, wi_s = split_misc(misc_s)

    rep = lambda a: jnp.repeat(a, DH, axis=-1)
    alog = A_log[0].astype(F32)
    dtb = dt_bias[0].astype(F32)
    n_chunks = t // CHUNK
    o_g_p, s_p = _gdn_prompt(
        qkv, rep(a_p), rep(b_p), a_p.reshape(n_chunks, CHUNK, HEADS).swapaxes(1, 2), z,
        conv_w[0].astype(F32), rep(alog)[None, :], rep(dtb)[None, :],
        jnp.broadcast_to(alog[:, None], (HEADS, CHUNK)), jnp.broadcast_to(dtb[:, None], (HEADS, CHUNK)),
        _tile_heads(gdn_norm_g[0]), bd, ltri, utri, eye)
    conv_p = qkv[t - (CONV_WIDTH - 1):t]

    ext = jnp.concatenate([state_conv[0].astype(F32), qkv_s[:, None, :]], axis=1)
    ext_t = ext.transpose(1, 2, 0).reshape(CONV_WIDTH, 3, HEADS, DH, nb)
    cw_t = jnp.broadcast_to(conv_w[0].astype(F32).reshape(CONV_WIDTH, 3, HEADS, DH, 1), ext_t.shape)
    row_t = lambda a: jnp.broadcast_to(a, (HEADS, nb)).reshape(HEADS, 1, nb)
    o_g_st, s_st = _gdn_step(
        ext_t[:, 0], ext_t[:, 1], ext_t[:, 2], cw_t[:, 0], cw_t[:, 1], cw_t[:, 2],
        row_t(a_s.T), row_t(b_s.T), row_t(alog[:, None]), row_t(dtb[:, None]),
        z_s.T.reshape(HEADS, DH, nb), jnp.broadcast_to(gdn_norm_g[0].astype(F32)[:, None], (DH, nb)),
        state_gdn[0].astype(F32).transpose(1, 2, 3, 0))
    o_g_s = o_g_st.reshape(HW, nb).T
    s_s = s_st.transpose(3, 0, 1, 2)
    conv_s = ext[:, 1:]

    nkt = t // BK
    vt_aug = jnp.concatenate([v_bf.reshape(nkt, BK, HEADS, DH).transpose(0, 2, 3, 1),
                              jnp.ones((nkt, HEADS, V_ROWS - DH, BK), BF16)], axis=2)
    o_a_t = _dsa_prompt(
        ki_f.astype(BF16).reshape(nkt, BK, DH), qi_bf.T, wi_p.T, q_bf.T,
        k_bf.reshape(nkt, BK, HW), vt_aug, min(TOPK_MAX, t // 4))
    o_a_p = o_a_t.T

    n_pool, page = cache_idx_k.shape[1], cache_idx_k.shape[2]
    past = page_table.shape[1] * page
    pt = page_table.astype(I32)
    mask, new_sel = _dsa_select(
        pt, cache_idx_k[0].astype(F32).transpose(0, 2, 1), qi_s.reshape(nb, HEADS, DH),
        jnp.broadcast_to(wi_s[:, :, None], (nb, HEADS, page)), ki_s[:, None, :], min(TOPK_MAX, (past + 1) // 4))
    col = lambda a: a.reshape(nb, HEADS, DH, 1)
    o_a_s = _dsa_attend(
        pt, cache_k[0].astype(F32).transpose(0, 2, 3, 1), cache_v[0].astype(F32).transpose(0, 2, 3, 1),
        mask, new_sel, col(q_s), col(k_s), col(v_s)).reshape(nb, HW)

    mix_p = jnp.concatenate([o_g_p, o_a_p.astype(BF16)], axis=1)
    mix_s = jnp.concatenate([o_g_s, o_a_s], axis=1)
    w_router = jnp.concatenate([
        w_router_group[0], w_router_expert[0].transpose(1, 0, 2).reshape(D_MODEL, N_EXPERTS),
        jnp.zeros((D_MODEL, LANES - N_GROUPS - N_EXPERTS), F32)], axis=1).astype(F32)
    wr1 = w_router.astype(BF16)
    wr2 = (w_router - wr1.astype(F32)).astype(BF16)
    wr3 = (w_router - wr1.astype(F32) - wr2.astype(F32)).astype(BF16)
    experts = (w_gate[0].astype(BF16), w_up[0].astype(BF16), w_down[0].astype(BF16))
    y_p = _finish(x_p, mix_p, w_out[0].astype(BF16), ffn_norm_g.astype(F32), wr1, wr2, wr3, *experts,
                  tm=TM_FIN, exact=False)
    y_s = _finish(x_s, mix_s, w_out[0].astype(F32), ffn_norm_g.astype(F32), wr1, wr2, wr3, *experts,
                  tm=nb, exact=True)

    return (y_p[None], y_s[:, None, :],
            k_f.reshape(1, 1, t, HEADS, DH), v_f.reshape(1, 1, t, HEADS, DH), ki_f[None, None],
            s_p[None, None], conv_p[None, None],
            k_s.reshape(1, nb, 1, HEADS, DH), v_s.reshape(1, nb, 1, HEADS, DH),
            ki_s.reshape(1, nb, 1, DH), s_s[None], conv_s[None])
```

```python
import functools

import jax
import jax.numpy as jnp
import numpy as np
from jax import lax
from jax.experimental import pallas as pl
from jax.experimental.pallas import tpu as pltpu

F32 = jnp.float32
BF16 = jnp.bfloat16
I32 = jnp.int32
EPS = 1e-6
NEG = -1e30
INT_MIN = -(2 ** 31)

D_MODEL = 1024
HEADS = 8
DH = 64
HW = HEADS * DH
CONV_DIM = 3 * HW
CONV_WIDTH = 4
CHUNK = 64
TOPK_MAX = 256
N_GROUPS = 4
EPG = 8
N_EXPERTS = N_GROUPS * EPG
D_EXPERT = 256
LANES = 128
IN_SPLITS = (CONV_DIM, HW, HEADS, HEADS, HW, HW, HW, HW, DH, HEADS)

TM_IN = 256
TM_FIN = 512
EXPERTS_PER_STEP = 4
BQ = 256
BK = 256
KV_TILES_PER_BLOCK = 8
V_ROWS = DH + 16
BRACKET_BITS = 13
BISECT_STEPS_PER_CHECK = 4
LOG2E = 1.4426950408889634
PAGES_PER_STEP = 16
VMEM_LIMIT = 52 * 1024 * 1024


def _cparams(sem):
    return pltpu.CompilerParams(dimension_semantics=sem, vmem_limit_bytes=VMEM_LIMIT)


def _sigmoid(x):
    return 1.0 / (1.0 + jnp.exp(-x))


def _softplus(x):
    return jnp.maximum(x, 0.0) + jnp.log(1.0 + jnp.exp(-jnp.abs(x)))


def _mm(a, b):
    return jnp.dot(a.astype(BF16), b.astype(BF16), preferred_element_type=F32)


def _mm_nt(a, b):
    return lax.dot_general(a.astype(BF16), b.astype(BF16), (((1,), (1,)), ((), ())),
                           preferred_element_type=F32)


def _split2(x):
    hi = x.astype(BF16)
    return hi, (x - hi.astype(F32)).astype(BF16)


def _mm3(a, b):
    ah, al = _split2(a)
    bh, bl = _split2(b)
    d = lambda x, y: jnp.dot(x, y, preferred_element_type=F32)
    return d(ah, bh) + d(ah, bl) + d(al, bh)


_mm_inv = _mm3


def _mm_nt3(a, b):
    ah, al = _split2(a)
    bh, bl = _split2(b)
    d = lambda x, y: lax.dot_general(x, y, (((1,), (1,)), ((), ())), preferred_element_type=F32)
    return d(ah, bh) + d(ah, bl) + d(al, bh)


def _split3(x):
    x1 = x.astype(BF16)
    r = x - x1.astype(F32)
    x2 = r.astype(BF16)
    x3 = (r - x2.astype(F32)).astype(BF16)
    return x1, x2, x3


def _dot_ones_l(ones_bf, x):
    x1, x2, x3 = _split3(x)
    d = lambda p: jnp.dot(ones_bf, p, preferred_element_type=F32)
    return d(x1) + d(x2) + d(x3)


def _dot_ones_r(x, ones_bf):
    x1, x2, x3 = _split3(x)
    d = lambda p: jnp.dot(p, ones_bf, preferred_element_type=F32)
    return d(x1) + d(x2) + d(x3)


def _head_sum(y, bd):
    return _dot_ones_r(y, bd)


def _inproj_body(x_ref, g_ref, w_ref, bd_ref, qg_ref, kg_ref, ikg_ref,
                 qkv_ref, z_ref, q_ref, k_ref, kb_ref, v_ref, vb_ref, qi_ref, ki_ref, misc_ref, *, exact):
    x = x_ref[...]
    ms = jnp.mean(x * x, axis=-1, keepdims=True)
    h = x * lax.rsqrt(ms + EPS) * g_ref[...]
    if not exact:
        h = h.astype(BF16)

    def seg(lo, n):
        if exact:
            return jnp.dot(h, w_ref[:, lo:lo + n], precision=lax.Precision.HIGHEST, preferred_element_type=F32)
        return jnp.dot(h, w_ref[:, lo:lo + n], preferred_element_type=F32)

    for j in range(3):
        qkv_ref[:, j * HW:(j + 1) * HW] = seg(j * HW, HW)
    z_ref[...] = seg(3 * HW, HW)
    bd = bd_ref[...]

    def head_rms(y, gain):
        return y * lax.rsqrt(_head_sum(y * y, bd) * (1.0 / DH) + EPS) * gain

    q = head_rms(seg(4 * HW, HW), qg_ref[...])
    q_ref[...] = (q * (DH ** -0.5 * (1.0 if exact else LOG2E))).astype(q_ref.dtype)
    k = head_rms(seg(5 * HW, HW), kg_ref[...])
    k_ref[...] = k
    kb_ref[...] = k.astype(BF16)
    v = seg(6 * HW, HW)
    v_ref[...] = v
    vb_ref[...] = v.astype(BF16)
    qi_ref[...] = seg(7 * HW, HW).astype(qi_ref.dtype)
    tail = seg(8 * HW, LANES)
    ki = tail[:, :DH]
    ki_ms = jnp.mean(ki * ki, axis=-1, keepdims=True)
    ki_ref[...] = ki * lax.rsqrt(ki_ms + EPS) * ikg_ref[...]
    misc_ref[...] = tail


def _inproj(x_all, attn_g, w_r, bd, qg, kg, ikg, tm, exact):
    tp = x_all.shape[0]
    nw = w_r.shape[1]
    qdt = F32 if exact else BF16
    row = lambda w: pl.BlockSpec((tm, w), lambda i: (i, 0))
    full = lambda a: pl.BlockSpec(a.shape, lambda i: (0,) * a.ndim)
    out_shapes = [
        jax.ShapeDtypeStruct((tp, CONV_DIM), F32),
        jax.ShapeDtypeStruct((tp, HW), F32),
        jax.ShapeDtypeStruct((tp, HW), qdt),
        jax.ShapeDtypeStruct((tp, HW), F32),
        jax.ShapeDtypeStruct((tp, HW), BF16),
        jax.ShapeDtypeStruct((tp, HW), F32),
        jax.ShapeDtypeStruct((tp, HW), BF16),
        jax.ShapeDtypeStruct((tp, HW), qdt),
        jax.ShapeDtypeStruct((tp, DH), F32),
        jax.ShapeDtypeStruct((tp, LANES), F32),
    ]
    out_specs = [row(CONV_DIM), row(HW), row(HW), row(HW), row(HW), row(HW), row(HW), row(HW),
                 row(DH), row(LANES)]
    return pl.pallas_call(
        functools.partial(_inproj_body, exact=exact),
        grid=(tp // tm,),
        in_specs=[row(D_MODEL), full(attn_g), pl.BlockSpec((D_MODEL, nw), lambda i: (0, 0)), full(bd),
                  full(qg), full(kg), full(ikg)],
        out_specs=out_specs,
        out_shape=out_shapes,
        compiler_params=_cparams(("parallel",)),
        name="inproj_exact" if exact else "inproj",
    )(x_all, attn_g, w_r, bd, qg, kg, ikg)


def _gdn_body(qkv_ref, ae_ref, be_ref, at_ref, z_ref, cw_ref, alog_ref, dtb_ref, alogt_ref, dtbt_ref,
              ng_ref, bd_ref, ltri_ref, utri_ref, eye_ref,
              o_ref, s_out_ref, xb, s_scr):
    c = pl.program_id(0)

    @pl.when(c == 0)
    def _():
        xb[0:8, :] = jnp.zeros((8, CONV_DIM), F32)
        s_scr[...] = jnp.zeros_like(s_scr)

    xb[8:8 + CHUNK, :] = qkv_ref[...]
    base = 8 - (CONV_WIDTH - 1)
    conv = xb[base:base + CHUNK, :] * cw_ref[0:1, :]
    for i in range(1, CONV_WIDTH):
        conv = conv + xb[base + i:base + i + CHUNK, :] * cw_ref[i:i + 1, :]
    xb[0:8, :] = xb[CHUNK:CHUNK + 8, :]
    act = conv * _sigmoid(conv)
    q = act[:, 0:HW]
    k = act[:, HW:2 * HW]
    v = act[:, 2 * HW:3 * HW]
    bd = bd_ref[...]
    qn = q * lax.rsqrt(_head_sum(q * q, bd) + EPS) * (DH ** -0.5)
    kn = k * lax.rsqrt(_head_sum(k * k, bd) + EPS)
    beta = _sigmoid(be_ref[...])
    g = -jnp.exp(alog_ref[...]) * _softplus(ae_ref[...] + dtb_ref[...])
    gc = _dot_ones_l(ltri_ref[...], g)
    gt = -jnp.exp(alogt_ref[...]) * _softplus(at_ref[0] + dtbt_ref[...])
    gct = _dot_ones_r(gt, utri_ref[...])
    gl = gc[CHUNK - 1:CHUNK, :]
    eg = jnp.exp(gc)
    kb = kn * beta
    vb = v * beta
    kbe = kb * eg
    qe = qn * eg
    kdec = kn * jnp.exp(gl - gc)
    egl = jnp.exp(gl)
    ri = lax.broadcasted_iota(I32, (CHUNK, CHUNK), 0)
    ci = lax.broadcasted_iota(I32, (CHUNK, CHUNK), 1)
    causal = ri >= ci
    strict = ri > ci
    eye = eye_ref[...]
    heads = range(HEADS)
    hsl = [slice(DH * h, DH * (h + 1)) for h in heads]
    dec = [jnp.exp(jnp.where(causal, gc[:, hsl[h]] - gct[h:h + 1, :], NEG)) for h in heads]
    kh = [kn[:, hsl[h]].astype(BF16) for h in heads]
    p = [-jnp.where(strict, _mm_nt(kb[:, hsl[h]], kh[h]) * dec[h], 0.0) for h in heads]
    attn = [_mm_nt(qn[:, hsl[h]], kh[h]) * dec[h] for h in heads]
    kdt = [_mm_nt(eye, kdec[:, hsl[h]]) for h in heads]
    xx = [jnp.concatenate([vb[:, hsl[h]], kbe[:, hsl[h]]], axis=1) for h in heads]
    for r in range(6):
        xx = [xx[h] + _mm_inv(p[h], xx[h]) for h in heads]
        if r < 5:
            p = [_mm_inv(p[h], p[h]) for h in heads]
    sh = [s_scr[h] for h in heads]
    v_new = [xx[h][:, :DH] - _mm(xx[h][:, DH:], sh[h]) for h in heads]
    outs = [_mm(qe[:, hsl[h]], sh[h]) + _mm(attn[h], v_new[h]) for h in heads]
    for h in heads:
        s_scr[h] = sh[h] * egl[:, hsl[h]] + _mm(kdt[h], v_new[h])
    o = jnp.concatenate(outs, axis=1)
    on = o * lax.rsqrt(_head_sum(o * o, bd) * (1.0 / DH) + EPS) * ng_ref[...]
    z = z_ref[...]
    o_ref[...] = (on * (z * _sigmoid(z))).astype(BF16)

    @pl.when(c == pl.num_programs(0) - 1)
    def _():
        s_out_ref[...] = s_scr[...]


def _gdn_prompt(qkv, a_e, b_e, a_t, z, conv_w, alog_e, dtb_e, alog_t, dtb_t, ng, bd, ltri, utri, eye):
    t = qkv.shape[0]
    row = lambda w: pl.BlockSpec((CHUNK, w), lambda i: (i, 0))
    full = lambda a: pl.BlockSpec(a.shape, lambda i: (0,) * a.ndim)
    return pl.pallas_call(
        _gdn_body,
        grid=(t // CHUNK,),
        in_specs=[row(CONV_DIM), row(HW), row(HW), pl.BlockSpec((1, HEADS, CHUNK), lambda i: (i, 0, 0)), row(HW),
                  full(conv_w), full(alog_e), full(dtb_e), full(alog_t), full(dtb_t), full(ng), full(bd),
                  full(ltri), full(utri), full(eye)],
        out_specs=[row(HW), pl.BlockSpec((HEADS, DH, DH), lambda i: (0, 0, 0))],
        out_shape=[jax.ShapeDtypeStruct((t, HW), BF16), jax.ShapeDtypeStruct((HEADS, DH, DH), F32)],
        scratch_shapes=[pltpu.VMEM((CHUNK + 8, CONV_DIM), F32), pltpu.VMEM((HEADS, DH, DH), F32)],
        compiler_params=_cparams(("arbitrary",)),
        name="gdn_prompt",
    )(qkv, a_e, b_e, a_t, z, conv_w, alog_e, dtb_e, alog_t, dtb_t, ng, bd, ltri, utri, eye)


def _gdn_step_body(xq_ref, xk_ref, xv_ref, cq_ref, ck_ref, cv_ref, a_ref, b_ref, alog_ref, dtb_ref, z_ref, ng_ref,
                   s0_ref, o_ref, s1_ref):
    def conv(x_ref, c_ref):
        acc = x_ref[0, 0] * c_ref[0, 0]
        for i in range(1, CONV_WIDTH):
            acc = acc + x_ref[i, 0] * c_ref[i, 0]
        return acc * _sigmoid(acc)

    q = conv(xq_ref, cq_ref)
    k = conv(xk_ref, ck_ref)
    v = conv(xv_ref, cv_ref)
    qn = q * lax.rsqrt(jnp.sum(q * q, axis=0, keepdims=True) + EPS) * (DH ** -0.5)
    kn = k * lax.rsqrt(jnp.sum(k * k, axis=0, keepdims=True) + EPS)
    beta = _sigmoid(b_ref[0])
    eg = jnp.exp(-jnp.exp(alog_ref[0]) * _softplus(a_ref[0] + dtb_ref[0]))
    ks = jnp.zeros_like(v)
    for d in range(DH):
        ks = ks + kn[d:d + 1, :] * s0_ref[0, d]
    delta = (v - ks * eg) * beta
    o = jnp.zeros_like(v)
    for d in range(DH):
        s_new = s0_ref[0, d] * eg + kn[d:d + 1, :] * delta
        s1_ref[0, d] = s_new
        o = o + qn[d:d + 1, :] * s_new
    on = o * lax.rsqrt(jnp.mean(o * o, axis=0, keepdims=True) + EPS) * ng_ref[...]
    z = z_ref[0]
    o_ref[0] = on * (z * _sigmoid(z))


def _gdn_step(xq, xk, xv, cq, ck, cv, a_t, b_t, alog_t, dtb_t, z_t, ng_t, s0_t):
    b = xq.shape[-1]
    x_spec = pl.BlockSpec((CONV_WIDTH, 1, DH, b), lambda h: (0, h, 0, 0))
    r_spec = pl.BlockSpec((1, 1, b), lambda h: (h, 0, 0))
    s_spec = pl.BlockSpec((1, DH, DH, b), lambda h: (h, 0, 0, 0))
    hd_spec = pl.BlockSpec((1, DH, b), lambda h: (h, 0, 0))
    return pl.pallas_call(
        _gdn_step_body,
        grid=(HEADS,),
        in_specs=[x_spec, x_spec, x_spec, x_spec, x_spec, x_spec, r_spec, r_spec, r_spec, r_spec, hd_spec,
                  pl.BlockSpec((DH, b), lambda h: (0, 0)), s_spec],
        out_specs=[hd_spec, s_spec],
        out_shape=[jax.ShapeDtypeStruct((HEADS, DH, b), F32), jax.ShapeDtypeStruct((HEADS, DH, DH, b), F32)],
        compiler_params=_cparams(("parallel",)),
        name="gdn_step",
    )(xq, xk, xv, cq, ck, cv, a_t, b_t, alog_t, dtb_t, z_t, ng_t, s0_t)


def _count_tiles(sc, n_tiles, pred):
    def body(kt, cnt):
        hit = jnp.where(pred(kt, sc[kt]), 1, 0).astype(I32)
        return cnt + jnp.sum(hit.reshape(BK // 8, 8, BQ), axis=0)

    cnt = lax.fori_loop(0, n_tiles, body, jnp.zeros((8, BQ), I32))
    return jnp.sum(cnt, axis=0, keepdims=True)


def _count_ge(sc, n_tiles, cand):
    return _count_tiles(sc, n_tiles, lambda kt, s: s >= cand)


def _key_to_f32(key):
    f = pltpu.bitcast(jnp.where(key < 0, key ^ jnp.int32(0x7FFFFFFF), key), F32)
    return jnp.where(key <= INT_MIN + 0x7FFFFF, -jnp.inf, f)


def _greedy_key(count_fn, k, shape, bits):
    lo = jnp.where(count_fn(jnp.zeros(shape, F32)) >= k, 0, INT_MIN).astype(I32)

    def bit(i, lo):
        cand = lo + jnp.left_shift(jnp.int32(1), 30 - i)
        return jnp.where(count_fn(_key_to_f32(cand)) >= k, cand, lo)

    return lax.fori_loop(0, bits, bit, lo)


def _kth_largest(count_fn, k, shape):
    return _key_to_f32(_greedy_key(count_fn, k, shape, 31))


def _kth_largest_bracketed(count_fn, k, lo, hi):
    def unresolved(lo, hi, c_lo):
        gap = hi - lo
        return ((gap > 1) | (gap < 0)) & (c_lo != k)

    def cond(st):
        i, lo, hi, c_lo = st
        return (i < 34) & (jnp.max(jnp.where(unresolved(lo, hi, c_lo), 1, 0)) > 0)

    def body(st):
        i, lo, hi, c_lo = st
        for _ in range(BISECT_STEPS_PER_CHECK):
            mid = lo + lax.shift_right_logical(hi - lo, 1)
            c = count_fn(_key_to_f32(mid))
            take = c >= k
            lo, hi, c_lo = jnp.where(take, mid, lo), jnp.where(take, hi, mid), jnp.where(take, c, c_lo)
        return i + BISECT_STEPS_PER_CHECK, lo, hi, c_lo

    _, lo, _, c_lo = lax.while_loop(cond, body, (jnp.int32(0), lo, hi, count_fn(_key_to_f32(lo))))
    return lo, c_lo


def _dsa_body(kib_ref, qit_ref, wt_ref, qt_ref, k_ref, vt_ref, o_ref, sc, gm, thr, m_s, a_s, lg_s, acc, qz, *,
              topk, idx_bits):
    qb = pl.program_id(0)
    kb = pl.program_id(1)
    q0 = qb * BQ
    n_tiles = (q0 + BQ + BK - 1) // BK
    qpos = q0 + lax.broadcasted_iota(I32, (BK, BQ), 1)
    rowi = lax.broadcasted_iota(I32, (BK, BQ), 0)

    @pl.when(kb == 0)
    def _score():
        w = wt_ref[...]

        def tile(kt, carry):
            ki_t = kib_ref[kt]
            s = jnp.zeros((BK, BQ), F32)
            for h in range(HEADS):
                sh = jnp.dot(ki_t, qit_ref[DH * h:DH * (h + 1), :], preferred_element_type=F32)
                s = s + jnp.maximum(sh, 0.0) * w[h:h + 1, :]
            vis = (kt * BK + rowi) <= qpos
            s = jnp.where(vis, s, -jnp.inf)
            sc[kt] = s
            gm[pl.ds(pl.multiple_of(kt * 8, 8), 8), :] = jnp.max(s.reshape(BK // 8, 8, BQ), axis=0)
            return carry

        gm[...] = jnp.full_like(gm, -jnp.inf)
        lax.fori_loop(0, n_tiles, tile, 0)

        def gm_count(cand):
            hit = jnp.where(gm[...] >= cand, 1, 0).astype(I32)
            return jnp.sum(jnp.sum(hit.reshape(gm.shape[0] // 8, 8, BQ), axis=0), axis=0, keepdims=True)

        k_up = max(topk // (BK // 8), 1)
        lo = _greedy_key(gm_count, topk, (1, BQ), BRACKET_BITS)
        hi = _greedy_key(gm_count, k_up, (1, BQ), BRACKET_BITS) + (1 << (31 - BRACKET_BITS))
        key, c_key = _kth_largest_bracketed(functools.partial(_count_ge, sc, n_tiles), topk, lo, hi)
        t = _key_to_f32(key)
        thr[...] = t
        over = (c_key > topk) & (t > -jnp.inf)

        @pl.when(jnp.max(jnp.where(over, 1, 0)) > 0)
        def _break_ties():
            room = topk - _count_tiles(sc, n_tiles, lambda kt, s: s > t)

            def bit(i, x):
                cand = x + jnp.left_shift(jnp.int32(1), idx_bits - 1 - i)
                below = _count_tiles(sc, n_tiles, lambda kt, s: (s == t) & ((kt * BK + rowi) < cand))
                return jnp.where(below < room, cand, x)

            last_kept = lax.fori_loop(0, idx_bits, bit, jnp.zeros((1, BQ), I32))

            def drop(kt, carry):
                s = sc[kt]
                sc[kt] = jnp.where((s == t) & ((kt * BK + rowi) > last_kept), -jnp.inf, s)
                return carry

            lax.fori_loop(0, n_tiles, drop, 0)

        m_s[...] = jnp.full_like(m_s, NEG)
        acc[...] = jnp.zeros_like(acc)
        zero = jnp.zeros((DH, BQ), BF16)
        for h in range(HEADS):
            qh = qt_ref[DH * h:DH * (h + 1), :]
            qz[h, 0:DH, :] = qh if h % 2 == 0 else zero
            qz[h, DH:2 * DH, :] = zero if h % 2 == 0 else qh

    @pl.when(kb * (KV_TILES_PER_BLOCK * BK) < q0 + BQ)
    def _attend():
        t0 = kb * KV_TILES_PER_BLOCK
        th = thr[...]

        def tile(j, carry):
            gt = t0 + j
            msk = (sc[gt] >= th) & ((gt * BK + rowi) <= qpos)
            bias = jnp.where(msk, 0.0, NEG)
            for h in range(HEADS):
                pair = LANES * (h // 2)
                lg = jnp.dot(k_ref[j, :, pair:pair + LANES], qz[h], preferred_element_type=F32) + bias
                lg_s[h] = lg
                m_old = m_s[h]
                m_new = jnp.maximum(m_old, jnp.max(lg, axis=0, keepdims=True))
                a_s[h] = jnp.exp2(m_old - m_new)
                m_s[h] = m_new
            for h in range(HEADS):
                p = jnp.exp2(lg_s[h] - m_s[h])
                pv = jnp.dot(vt_ref[j, h], p.astype(BF16), preferred_element_type=F32)
                acc[h] = a_s[h] * acc[h] + pv
            return carry

        lax.fori_loop(0, jnp.minimum(KV_TILES_PER_BLOCK, n_tiles - t0), tile, 0)

    @pl.when(kb == pl.num_programs(1) - 1)
    def _finish():
        for h in range(HEADS):
            o_ref[DH * h:DH * (h + 1), :] = acc[h, 0:DH, :] / acc[h, DH:DH + 1, :]


def _dsa_prompt(kib3, qit, wt, qt, k3, vt3, topk):
    nkt = kib3.shape[0]
    t = nkt * BK
    nq = t // BQ
    nkb = -(-nkt // KV_TILES_PER_BLOCK)
    kvb = min(KV_TILES_PER_BLOCK, nkt)
    span = kvb * BK

    def kv_idx(qb, kb):
        return (jnp.minimum(kb, (qb * BQ + BQ - 1) // span), 0, 0)

    return pl.pallas_call(
        functools.partial(_dsa_body, topk=topk, idx_bits=(t - 1).bit_length()),
        grid=(nq, nkb),
        in_specs=[pl.BlockSpec((nkt, BK, DH), lambda qb, kb: (0, 0, 0)),
                  pl.BlockSpec((HW, BQ), lambda qb, kb: (0, qb)),
                  pl.BlockSpec((HEADS, BQ), lambda qb, kb: (0, qb)),
                  pl.BlockSpec((HW, BQ), lambda qb, kb: (0, qb)),
                  pl.BlockSpec((kvb, BK, HW), kv_idx),
                  pl.BlockSpec((kvb, HEADS, V_ROWS, BK), lambda qb, kb: kv_idx(qb, kb) + (0,))],
        out_specs=pl.BlockSpec((HW, BQ), lambda qb, kb: (0, qb)),
        out_shape=jax.ShapeDtypeStruct((HW, t), F32),
        scratch_shapes=[pltpu.VMEM((nkt, BK, BQ), F32), pltpu.VMEM((nkt * 8, BQ), F32), pltpu.VMEM((1, BQ), F32),
                        pltpu.VMEM((HEADS, 1, BQ), F32), pltpu.VMEM((HEADS, 1, BQ), F32),
                        pltpu.VMEM((HEADS, BK, BQ), F32), pltpu.VMEM((HEADS, V_ROWS, BQ), F32),
                        pltpu.VMEM((HEADS, 2 * DH, BQ), BF16)],
        compiler_params=_cparams(("parallel", "arbitrary")),
        name="dsa_prompt",
    )(kib3, qit, wt, qt, k3, vt3)


def _prefix_count(mf, upper, lower_strict):
    within = jnp.dot(mf.astype(BF16), upper, preferred_element_type=F32)
    tot = jnp.broadcast_to(within[:, -1:], within.shape)
    return within + jnp.dot(lower_strict, tot.astype(BF16), preferred_element_type=F32)


def _dsa_select_body(pt_ref, *refs, n_chunks, page, topk):
    pp = PAGES_PER_STEP
    ik_refs = refs[0:pp]
    qi_ref, w_ref, kin_ref, mask_ref, meta_ref, sc = refs[pp:]
    del pt_ref
    c = pl.program_id(1)
    n_pages = n_chunks * pp
    qi = qi_ref[...]
    w = w_ref[...]
    pages = jnp.concatenate([r[...] for r in ik_refs], axis=1)
    s8 = _mm3(qi, pages)
    srow = jnp.sum(jnp.maximum(s8, 0.0) * pltpu.repeat(w, pp, axis=1), axis=0, keepdims=True)
    sc[c] = jnp.concatenate([srow[:, i * page:(i + 1) * page] for i in range(pp)], axis=0)

    @pl.when(c == n_chunks - 1)
    def _select():
        s_new = jnp.sum(qi * kin_ref[...], axis=1, keepdims=True)
        s_new = jnp.sum(jnp.maximum(s_new, 0.0) * w[:, 0:1], axis=0, keepdims=True)
        total = lambda m: jnp.sum(jnp.sum(m, axis=1, keepdims=True), axis=0, keepdims=True)
        s_all = sc[...].reshape(n_pages, page)

        def count(cand):
            return total(jnp.where(s_all >= cand, 1.0, 0.0)) + jnp.where(s_new >= cand, 1.0, 0.0)

        thr = _kth_largest(count, topk, (1, 1))
        ri = lax.broadcasted_iota(I32, (page, page), 0)
        ci = lax.broadcasted_iota(I32, (page, page), 1)
        upper = jnp.where(ri <= ci, 1.0, 0.0).astype(BF16)
        rp = lax.broadcasted_iota(I32, (n_pages, n_pages), 0)
        cp = lax.broadcasted_iota(I32, (n_pages, n_pages), 1)
        lower_strict = jnp.where(cp < rp, 1.0, 0.0).astype(BF16)
        gt = s_all > thr
        eq = s_all == thr
        eqf = jnp.where(eq, 1.0, 0.0)
        room = topk - total(jnp.where(gt, 1.0, 0.0)) - jnp.where(s_new > thr, 1.0, 0.0)
        sel = gt | (eq & (_prefix_count(eqf, upper, lower_strict) <= room))
        new_sel = (s_new > thr) | ((s_new == thr) & (total(eqf) < room))
        mask_ref[...] = jnp.where(sel, 1.0, 0.0)
        meta_ref[...] = jnp.where(new_sel, 1.0, 0.0) * jnp.ones(meta_ref.shape, F32)


def _dsa_select(page_table, cik_t, qi_s, w_s, ki_new, topk):
    b, n_pages = page_table.shape
    page = cik_t.shape[2]
    pp = PAGES_PER_STEP
    n_chunks = n_pages // pp

    def ik_spec(i):
        return pl.BlockSpec((None, DH, page), lambda bb, c, pt: (pt[bb, c * pp + i], 0, 0))

    per_b = lambda r, w: pl.BlockSpec((None, r, w), lambda bb, c, pt: (bb, 0, 0))
    grid_spec = pltpu.PrefetchScalarGridSpec(
        num_scalar_prefetch=1,
        grid=(b, n_chunks),
        in_specs=[ik_spec(i) for i in range(pp)] + [per_b(HEADS, DH), per_b(HEADS, page), per_b(1, DH)],
        out_specs=[per_b(n_pages, page), per_b(8, page)],
        scratch_shapes=[pltpu.VMEM((n_chunks, pp, page), F32)],
    )
    return pl.pallas_call(
        functools.partial(_dsa_select_body, n_chunks=n_chunks, page=page, topk=topk),
        grid_spec=grid_spec,
        out_shape=[jax.ShapeDtypeStruct((b, n_pages, page), F32), jax.ShapeDtypeStruct((b, 8, page), F32)],
        compiler_params=_cparams(("parallel", "arbitrary")),
        name="dsa_select",
    )(page_table, *([cik_t] * pp), qi_s, w_s, ki_new)


def _dsa_attend_body(pt_ref, *refs, n_chunks):
    del pt_ref
    pp = PAGES_PER_STEP
    k_refs = refs[0:pp]
    v_refs = refs[pp:2 * pp]
    mask_ref, new_ref, q_ref, kn_ref, vn_ref, o_ref, qb, m_s, l_s, acc = refs[2 * pp:]
    c = pl.program_id(1)

    @pl.when(c == 0)
    def _init():
        qb[...] = jnp.broadcast_to(q_ref[...], qb.shape)
        m_s[...] = jnp.full_like(m_s, NEG)
        l_s[...] = jnp.zeros_like(l_s)
        acc[...] = jnp.zeros_like(acc)

    for i in range(pp):
        sel_row = mask_ref[i:i + 1, :] > 0.0
        for h in range(HEADS):
            lg = jnp.sum(k_refs[i][h] * qb[h], axis=0, keepdims=True)
            lg = jnp.where(sel_row, lg, NEG)
            m_old = m_s[h]
            m_new = jnp.maximum(m_old, jnp.max(lg, axis=1, keepdims=True))
            alpha = jnp.exp(m_old - m_new)
            p = jnp.exp(lg - m_new)
            l_s[h] = alpha * l_s[h] + p
            acc[h] = alpha * acc[h] + p * v_refs[i][h]
            m_s[h] = m_new

    @pl.when(c == n_chunks - 1)
    def _finish():
        new_sel = new_ref[0:1, 0:1].reshape(1, 1, 1) > 0.0
        lg_new = jnp.where(new_sel, jnp.sum(q_ref[...] * kn_ref[...], axis=1, keepdims=True), NEG)
        m_old = m_s[...][:, :, 0:1]
        m_fin = jnp.maximum(m_old, lg_new)
        scale = jnp.exp(m_old - m_fin)
        p_new = jnp.exp(lg_new - m_fin)
        den = jnp.sum(l_s[...], axis=2, keepdims=True) * scale + p_new
        o_ref[...] = (jnp.sum(acc[...], axis=2, keepdims=True) * scale + p_new * vn_ref[...]) / den


def _dsa_attend(page_table, ck_t, cv_t, mask, new_sel, q_c, k_c, v_c):
    b, n_pages = page_table.shape
    page = ck_t.shape[3]
    pp = PAGES_PER_STEP
    n_chunks = n_pages // pp

    def kv_spec(i):
        return pl.BlockSpec((None, HEADS, DH, page), lambda bb, c, pt: (pt[bb, c * pp + i], 0, 0, 0))

    col = pl.BlockSpec((None, HEADS, DH, 1), lambda bb, c, pt: (bb, 0, 0, 0))
    grid_spec = pltpu.PrefetchScalarGridSpec(
        num_scalar_prefetch=1,
        grid=(b, n_chunks),
        in_specs=([kv_spec(i) for i in range(pp)] + [kv_spec(i) for i in range(pp)]
                  + [pl.BlockSpec((None, pp, page), lambda bb, c, pt: (bb, c, 0)),
                     pl.BlockSpec((None, 8, page), lambda bb, c, pt: (bb, 0, 0)), col, col, col]),
        out_specs=col,
        scratch_shapes=[pltpu.VMEM((HEADS, DH, page), F32), pltpu.VMEM((HEADS, 1, page), F32),
                        pltpu.VMEM((HEADS, 1, page), F32), pltpu.VMEM((HEADS, DH, page), F32)],
    )
    return pl.pallas_call(
        functools.partial(_dsa_attend_body, n_chunks=n_chunks),
        grid_spec=grid_spec,
        out_shape=jax.ShapeDtypeStruct((b, HEADS, DH, 1), F32),
        compiler_params=_cparams(("parallel", "arbitrary")),
        name="dsa_attend",
    )(page_table, *([ck_t] * pp), *([cv_t] * pp), mask, new_sel, q_c, k_c, v_c)


def _route(logits):
    lane = lax.broadcasted_iota(I32, logits.shape, 1)
    mx = lambda a: jnp.max(a, axis=-1, keepdims=True)
    sm = lambda a: jnp.sum(a, axis=-1, keepdims=True)
    first = lambda hit: jnp.min(jnp.where(hit, lane, LANES), axis=-1, keepdims=True)
    gmask = lane < N_GROUPS
    lg = jnp.where(gmask, logits, NEG)
    gex = jnp.where(gmask, jnp.exp(lg - mx(lg)), 0.0)
    gp = gex / sm(gex)
    g_prob = mx(gp)
    g_idx = first(gmask & (gp == g_prob))
    e_lo = N_GROUPS + EPG * g_idx
    emask = (lane >= e_lo) & (lane < e_lo + EPG)
    le = jnp.where(emask, logits, NEG)
    eex = jnp.where(emask, jnp.exp(le - mx(le)), 0.0)
    ep = jnp.where(emask, eex / sm(eex), -1.0)
    p1 = mx(ep)
    i1 = first(ep == p1)
    ep2 = jnp.where(lane == i1, -1.0, ep)
    p2 = mx(ep2)
    i2 = first(ep2 == p2)
    den = p1 + p2
    return jnp.where(lane == i1, g_prob * p1 / den, 0.0) + jnp.where(lane == i2, g_prob * p2 / den, 0.0)


def _finish_body(x_ref, mix_ref, wo_ref, g_ref, wr1_ref, wr2_ref, wr3_ref, wg_ref, wu_ref, wd_ref,
                 y_ref, h2b, cw, acc, *, exact):
    e = pl.program_id(1)

    @pl.when(e == 0)
    def _():
        if exact:
            proj = jnp.dot(mix_ref[...], wo_ref[...], precision=lax.Precision.HIGHEST, preferred_element_type=F32)
        else:
            proj = jnp.dot(mix_ref[...], wo_ref[...], preferred_element_type=F32)
        y1 = x_ref[...] + proj
        ms = jnp.mean(y1 * y1, axis=-1, keepdims=True)
        h2 = y1 * lax.rsqrt(ms + EPS) * g_ref[...]
        h2b[...] = h2.astype(BF16)
        acc[...] = y1
        a1, a2, a3 = _split3(h2)
        d = lambda a, b_ref: jnp.dot(a, b_ref[...], preferred_element_type=F32)
        logits = (d(a1, wr1_ref) + d(a1, wr2_ref) + d(a2, wr1_ref)
                  + d(a1, wr3_ref) + d(a2, wr2_ref) + d(a3, wr1_ref))
        cw[...] = _route(logits)

    hb = h2b[...]
    cwv = cw[...]
    lane = lax.broadcasted_iota(I32, cwv.shape, 1)
    for j in range(EXPERTS_PER_STEP):
        eid = e * EXPERTS_PER_STEP + j
        gte = jnp.dot(hb, wg_ref[j], preferred_element_type=F32)
        up = jnp.dot(hb, wu_ref[j], preferred_element_type=F32)
        cwe = jnp.sum(jnp.where(lane == N_GROUPS + eid, cwv, 0.0), axis=-1, keepdims=True)
        a = (gte * _sigmoid(gte)) * up * cwe
        acc[...] += jnp.dot(a.astype(BF16), wd_ref[j], preferred_element_type=F32)

    @pl.when(e == pl.num_programs(1) - 1)
    def _():
        y_ref[...] = acc[...]


def _finish(x_all, mix, wo, ffn_g, wr1, wr2, wr3, wg, wu, wd, tm, exact):
    tp = x_all.shape[0]
    eb = EXPERTS_PER_STEP
    row = lambda w: pl.BlockSpec((tm, w), lambda i, e: (i, 0))
    full = lambda a: pl.BlockSpec(a.shape, lambda i, e: (0,) * a.ndim)
    return pl.pallas_call(
        functools.partial(_finish_body, exact=exact),
        grid=(tp // tm, N_EXPERTS // eb),
        in_specs=[row(D_MODEL), row(2 * HW), full(wo), full(ffn_g), full(wr1), full(wr2), full(wr3),
                  pl.BlockSpec((eb, D_MODEL, D_EXPERT), lambda i, e: (e, 0, 0)),
                  pl.BlockSpec((eb, D_MODEL, D_EXPERT), lambda i, e: (e, 0, 0)),
                  pl.BlockSpec((eb, D_EXPERT, D_MODEL), lambda i, e: (e, 0, 0))],
        out_specs=row(D_MODEL),
        out_shape=jax.ShapeDtypeStruct((tp, D_MODEL), F32),
        scratch_shapes=[pltpu.VMEM((tm, D_MODEL), BF16), pltpu.VMEM((tm, LANES), F32),
                        pltpu.VMEM((tm, D_MODEL), F32)],
        compiler_params=_cparams(("parallel", "arbitrary")),
        name="finish_exact" if exact else "finish",
    )(x_all, mix, wo, ffn_g, wr1, wr2, wr3, wg, wu, wd)


def _tile_heads(g):
    return jnp.tile(g.astype(F32), HEADS)[None, :]


def kernel(x_prompt, x_sample, cache_k, cache_v, cache_idx_k, state_gdn, state_conv, page_table, attn_norm_g, w_in,
           conv_w, A_log, dt_bias, gdn_norm_g, q_norm_g, k_norm_g, idx_k_norm_g, w_out, ffn_norm_g, w_router_group,
           w_router_expert, w_gate, w_up, w_down):
    assert w_in.shape[0] == 1 and x_prompt.shape[0] == 1 and x_sample.shape[1] == 1
    t = x_prompt.shape[1]
    nb = x_sample.shape[0]
    assert t % BQ == 0 and t % CHUNK == 0 and BQ == BK and t % TM_FIN == 0

    offs = np.concatenate([[0], np.cumsum(IN_SPLITS)])
    seg = lambda i: w_in[0][:, offs[i]:offs[i + 1]]
    tail_pad = LANES - (DH + 3 * HEADS)
    w_r = jnp.concatenate([seg(0), seg(1), seg(4), seg(5), seg(6), seg(7), seg(8), seg(2), seg(3), seg(9),
                           jnp.zeros((D_MODEL, tail_pad), F32)], axis=1).astype(F32)
    hid = np.arange(HW) // DH
    bd = jnp.asarray(hid[:, None] == hid[None, :], BF16)
    tri = np.arange(CHUNK)
    ltri = jnp.asarray(tri[:, None] >= tri[None, :], BF16)
    utri = jnp.asarray(tri[:, None] <= tri[None, :], BF16)
    eye = jnp.asarray(tri[:, None] == tri[None, :], BF16)

    norm_args = (attn_norm_g.astype(F32), bd, _tile_heads(q_norm_g[0]), _tile_heads(k_norm_g[0]),
                 idx_k_norm_g.astype(F32))
    x_p = x_prompt[0].astype(F32)
    x_s = x_sample[:, 0].astype(F32)
    (qkv, z, q_bf, k_f, k_bf, v_f, v_bf, qi_bf, ki_f, misc) = _inproj(
        x_p, norm_args[0], w_r.astype(BF16), *norm_args[1:], tm=TM_IN, exact=False)
    (qkv_s, z_s, q_s, k_s, _, v_s, _, qi_s, ki_s, misc_s) = _inproj(
        x_s, norm_args[0], w_r, *norm_args[1:], tm=nb, exact=True)
    split_misc = lambda m: (m[:, DH:DH + HEADS], m[:, DH + HEADS:DH + 2 * HEADS],
                            m[:, DH + 2 * HEADS:DH + 3 * HEADS] * ((HEADS * DH) ** -0.5))
    a_p, b_p, wi_p = split_misc(misc)
    a_s, b_s, wi_s = split_misc(misc_s)

    rep = lambda a: jnp.repeat(a, DH, axis=-1)
    alog = A_log[0].astype(F32)
    dtb = dt_bias[0].astype(F32)
    n_chunks = t // CHUNK
    o_g_p, s_p = _gdn_prompt(
        qkv, rep(a_p), rep(b_p), a_p.reshape(n_chunks, CHUNK, HEADS).swapaxes(1, 2), z,
        conv_w[0].astype(F32), rep(alog)[None, :], rep(dtb)[None, :],
        jnp.broadcast_to(alog[:, None], (HEADS, CHUNK)), jnp.broadcast_to(dtb[:, None], (HEADS, CHUNK)),
        _tile_heads(gdn_norm_g[0]), bd, ltri, utri, eye)
    conv_p = qkv[t - (CONV_WIDTH - 1):t]

    ext = jnp.concatenate([state_conv[0].astype(F32), qkv_s[:, None, :]], axis=1)
    ext_t = ext.transpose(1, 2, 0).reshape(CONV_WIDTH, 3, HEADS, DH, nb)
    cw_t = jnp.broadcast_to(conv_w[0].astype(F32).reshape(CONV_WIDTH, 3, HEADS, DH, 1), ext_t.shape)
    row_t = lambda a: jnp.broadcast_to(a, (HEADS, nb)).reshape(HEADS, 1, nb)
    o_g_st, s_st = _gdn_step(
        ext_t[:, 0], ext_t[:, 1], ext_t[:, 2], cw_t[:, 0], cw_t[:, 1], cw_t[:, 2],
        row_t(a_s.T), row_t(b_s.T), row_t(alog[:, None]), row_t(dtb[:, None]),
        z_s.T.reshape(HEADS, DH, nb), jnp.broadcast_to(gdn_norm_g[0].astype(F32)[:, None], (DH, nb)),
        state_gdn[0].astype(F32).transpose(1, 2, 3, 0))
    o_g_s = o_g_st.reshape(HW, nb).T
    s_s = s_st.transpose(3, 0, 1, 2)
    conv_s = ext[:, 1:]

    nkt = t // BK
    vt_aug = jnp.concatenate([v_bf.reshape(nkt, BK, HEADS, DH).transpose(0, 2, 3, 1),
                              jnp.ones((nkt, HEADS, V_ROWS - DH, BK), BF16)], axis=2)
    o_a_t = _dsa_prompt(
        ki_f.astype(BF16).reshape(nkt, BK, DH), qi_bf.T, wi_p.T, q_bf.T,
        k_bf.reshape(nkt, BK, HW), vt_aug, min(TOPK_MAX, t // 4))
    o_a_p = o_a_t.T

    n_pool, page = cache_idx_k.shape[1], cache_idx_k.shape[2]
    past = page_table.shape[1] * page
    pt = page_table.astype(I32)
    mask, new_sel = _dsa_select(
        pt, cache_idx_k[0].astype(F32).transpose(0, 2, 1), qi_s.reshape(nb, HEADS, DH),
        jnp.broadcast_to(wi_s[:, :, None], (nb, HEADS, page)), ki_s[:, None, :], min(TOPK_MAX, (past + 1) // 4))
    col = lambda a: a.reshape(nb, HEADS, DH, 1)
    o_a_s = _dsa_attend(
        pt, cache_k[0].astype(F32).transpose(0, 2, 3, 1), cache_v[0].astype(F32).transpose(0, 2, 3, 1),
        mask, new_sel, col(q_s), col(k_s), col(v_s)).reshape(nb, HW)

    mix_p = jnp.concatenate([o_g_p, o_a_p.astype(BF16)], axis=1)
    mix_s = jnp.concatenate([o_g_s, o_a_s], axis=1)
    w_router = jnp.concatenate([
        w_router_group[0], w_router_expert[0].transpose(1, 0, 2).reshape(D_MODEL, N_EXPERTS),
        jnp.zeros((D_MODEL, LANES - N_GROUPS - N_EXPERTS), F32)], axis=1).astype(F32)
    wr1 = w_router.astype(BF16)
    wr2 = (w_router - wr1.astype(F32)).astype(BF16)
    wr3 = (w_router - wr1.astype(F32) - wr2.astype(F32)).astype(BF16)
    experts = (w_gate[0].astype(BF16), w_up[0].astype(BF16), w_down[0].astype(BF16))
    y_p = _finish(x_p, mix_p, w_out[0].astype(BF16), ffn_norm_g.astype(F32), wr1, wr2, wr3, *experts,
                  tm=TM_FIN, exact=False)
    y_s = _finish(x_s, mix_s, w_out[0].astype(F32), ffn_norm_g.astype(F32), wr1, wr2, wr3, *experts,
                  tm=nb, exact=True)

    return (y_p[None], y_s[:, None, :],
            k_f.reshape(1, 1, t, HEADS, DH), v_f.reshape(1, 1, t, HEADS, DH), ki_f[None, None],
            s_p[None, None], conv_p[None, None],
            k_s.reshape(1, nb, 1, HEADS, DH), v_s.reshape(1, nb, 1, HEADS, DH),
            ki_s.reshape(1, nb, 1, DH), s_s[None], conv_s[None])
```

```python
import functools

import jax
import jax.numpy as jnp
import numpy as np
from jax import lax
from jax.experimental import pallas as pl
from jax.experimental.pallas import tpu as pltpu

F32 = jnp.float32
BF16 = jnp.bfloat16
I32 = jnp.int32
EPS = 1e-6
NEG = -1e30
INT_MIN = -(2 ** 31)

D_MODEL = 1024
HEADS = 8
DH = 64
HW = HEADS * DH
CONV_DIM = 3 * HW
CONV_WIDTH = 4
CHUNK = 64
TOPK_MAX = 256
N_GROUPS = 4
EPG = 8
N_EXPERTS = N_GROUPS * EPG
D_EXPERT = 256
LANES = 128
IN_SPLITS = (CONV_DIM, HW, HEADS, HEADS, HW, HW, HW, HW, DH, HEADS)

TM_IN = 256
TM_FIN = 512
EXPERTS_PER_STEP = 4
BQ = 256
BK = 256
KV_TILES_PER_BLOCK = 8
V_ROWS = DH + 16
BRACKET_BITS = 13
BISECT_STEPS_PER_CHECK = 4
LOG2E = 1.4426950408889634
PAGES_PER_STEP = 16
VMEM_LIMIT = 52 * 1024 * 1024


def _cparams(sem):
    return pltpu.CompilerParams(dimension_semantics=sem, vmem_limit_bytes=VMEM_LIMIT)


def _sigmoid(x):
    return 1.0 / (1.0 + jnp.exp(-x))


def _softplus(x):
    return jnp.maximum(x, 0.0) + jnp.log(1.0 + jnp.exp(-jnp.abs(x)))


def _mm(a, b):
    return jnp.dot(a.astype(BF16), b.astype(BF16), preferred_element_type=F32)


def _mm_nt(a, b):
    return lax.dot_general(a.astype(BF16), b.astype(BF16), (((1,), (1,)), ((), ())),
                           preferred_element_type=F32)


def _split2(x):
    hi = x.astype(BF16)
    return hi, (x - hi.astype(F32)).astype(BF16)


def _mm3(a, b):
    ah, al = _split2(a)
    bh, bl = _split2(b)
    d = lambda x, y: jnp.dot(x, y, preferred_element_type=F32)
    return d(ah, bh) + d(ah, bl) + d(al, bh)


_mm_inv = _mm3


def _mm_nt3(a, b):
    ah, al = _split2(a)
    bh, bl = _split2(b)
    d = lambda x, y: lax.dot_general(x, y, (((1,), (1,)), ((), ())), preferred_element_type=F32)
    return d(ah, bh) + d(ah, bl) + d(al, bh)


def _split3(x):
    x1 = x.astype(BF16)
    r = x - x1.astype(F32)
    x2 = r.astype(BF16)
    x3 = (r - x2.astype(F32)).astype(BF16)
    return x1, x2, x3


def _dot_ones_l(ones_bf, x):
    x1, x2, x3 = _split3(x)
    d = lambda p: jnp.dot(ones_bf, p, preferred_element_type=F32)
    return d(x1) + d(x2) + d(x3)


def _dot_ones_r(x, ones_bf):
    x1, x2, x3 = _split3(x)
    d = lambda p: jnp.dot(p, ones_bf, preferred_element_type=F32)
    return d(x1) + d(x2) + d(x3)


def _head_sum(y, bd):
    return _dot_ones_r(y, bd)


def _inproj_body(x_ref, g_ref, w_ref, bd_ref, qg_ref, kg_ref, ikg_ref,
                 qkv_ref, z_ref, q_ref, k_ref, kb_ref, v_ref, vb_ref, qi_ref, ki_ref, misc_ref, *, exact):
    x = x_ref[...]
    ms = jnp.mean(x * x, axis=-1, keepdims=True)
    h = x * lax.rsqrt(ms + EPS) * g_ref[...]
    if not exact:
        h = h.astype(BF16)

    def seg(lo, n):
        if exact:
            return jnp.dot(h, w_ref[:, lo:lo + n], precision=lax.Precision.HIGHEST, preferred_element_type=F32)
        return jnp.dot(h, w_ref[:, lo:lo + n], preferred_element_type=F32)

    for j in range(3):
        qkv_ref[:, j * HW:(j + 1) * HW] = seg(j * HW, HW)
    z_ref[...] = seg(3 * HW, HW)
    bd = bd_ref[...]

    def head_rms(y, gain):
        return y * lax.rsqrt(_head_sum(y * y, bd) * (1.0 / DH) + EPS) * gain

    q = head_rms(seg(4 * HW, HW), qg_ref[...])
    q_ref[...] = (q * (DH ** -0.5 * (1.0 if exact else LOG2E))).astype(q_ref.dtype)
    k = head_rms(seg(5 * HW, HW), kg_ref[...])
    k_ref[...] = k
    kb_ref[...] = k.astype(BF16)
    v = seg(6 * HW, HW)
    v_ref[...] = v
    vb_ref[...] = v.astype(BF16)
    qi_ref[...] = seg(7 * HW, HW).astype(qi_ref.dtype)
    tail = seg(8 * HW, LANES)
    ki = tail[:, :DH]
    ki_ms = jnp.mean(ki * ki, axis=-1, keepdims=True)
    ki_ref[...] = ki * lax.rsqrt(ki_ms + EPS) * ikg_ref[...]
    misc_ref[...] = tail


def _inproj(x_all, attn_g, w_r, bd, qg, kg, ikg, tm, exact):
    tp = x_all.shape[0]
    nw = w_r.shape[1]
    qdt = F32 if exact else BF16
    row = lambda w: pl.BlockSpec((tm, w), lambda i: (i, 0))
    full = lambda a: pl.BlockSpec(a.shape, lambda i: (0,) * a.ndim)
    out_shapes = [
        jax.ShapeDtypeStruct((tp, CONV_DIM), F32),
        jax.ShapeDtypeStruct((tp, HW), F32),
        jax.ShapeDtypeStruct((tp, HW), qdt),
        jax.ShapeDtypeStruct((tp, HW), F32),
        jax.ShapeDtypeStruct((tp, HW), BF16),
        jax.ShapeDtypeStruct((tp, HW), F32),
        jax.ShapeDtypeStruct((tp, HW), BF16),
        jax.ShapeDtypeStruct((tp, HW), qdt),
        jax.ShapeDtypeStruct((tp, DH), F32),
        jax.ShapeDtypeStruct((tp, LANES), F32),
    ]
    out_specs = [row(CONV_DIM), row(HW), row(HW), row(HW), row(HW), row(HW), row(HW), row(HW),
                 row(DH), row(LANES)]
    return pl.pallas_call(
        functools.partial(_inproj_body, exact=exact),
        grid=(tp // tm,),
        in_specs=[row(D_MODEL), full(attn_g), pl.BlockSpec((D_MODEL, nw), lambda i: (0, 0)), full(bd),
                  full(qg), full(kg), full(ikg)],
        out_specs=out_specs,
        out_shape=out_shapes,
        compiler_params=_cparams(("parallel",)),
        name="inproj_exact" if exact else "inproj",
    )(x_all, attn_g, w_r, bd, qg, kg, ikg)


def _gdn_body(qkv_ref, ae_ref, be_ref, at_ref, z_ref, cw_ref, alog_ref, dtb_ref, alogt_ref, dtbt_ref,
              ng_ref, bd_ref, ltri_ref, utri_ref, eye_ref,
              o_ref, s_out_ref, xb, s_scr):
    c = pl.program_id(0)

    @pl.when(c == 0)
    def _():
        xb[0:8, :] = jnp.zeros((8, CONV_DIM), F32)
        s_scr[...] = jnp.zeros_like(s_scr)

    xb[8:8 + CHUNK, :] = qkv_ref[...]
    base = 8 - (CONV_WIDTH - 1)
    conv = xb[base:base + CHUNK, :] * cw_ref[0:1, :]
    for i in range(1, CONV_WIDTH):
        conv = conv + xb[base + i:base + i + CHUNK, :] * cw_ref[i:i + 1, :]
    xb[0:8, :] = xb[CHUNK:CHUNK + 8, :]
    act = conv * _sigmoid(conv)
    q = act[:, 0:HW]
    k = act[:, HW:2 * HW]
    v = act[:, 2 * HW:3 * HW]
    bd = bd_ref[...]
    qn = q * lax.rsqrt(_head_sum(q * q, bd) + EPS) * (DH ** -0.5)
    kn = k * lax.rsqrt(_head_sum(k * k, bd) + EPS)
    beta = _sigmoid(be_ref[...])
    g = -jnp.exp(alog_ref[...]) * _softplus(ae_ref[...] + dtb_ref[...])
    gc = _dot_ones_l(ltri_ref[...], g)
    gt = -jnp.exp(alogt_ref[...]) * _softplus(at_ref[0] + dtbt_ref[...])
    gct = _dot_ones_r(gt, utri_ref[...])
    gl = gc[CHUNK - 1:CHUNK, :]
    eg = jnp.exp(gc)
    kb = kn * beta
    vb = v * beta
    kbe = kb * eg
    qe = qn * eg
    kdec = kn * jnp.exp(gl - gc)
    egl = jnp.exp(gl)
    ri = lax.broadcasted_iota(I32, (CHUNK, CHUNK), 0)
    ci = lax.broadcasted_iota(I32, (CHUNK, CHUNK), 1)
    causal = ri >= ci
    strict = ri > ci
    eye = eye_ref[...]
    heads = range(HEADS)
    hsl = [slice(DH * h, DH * (h + 1)) for h in heads]
    dec = [jnp.exp(jnp.where(causal, gc[:, hsl[h]] - gct[h:h + 1, :], NEG)) for h in heads]
    kh = [kn[:, hsl[h]].astype(BF16) for h in heads]
    p = [-jnp.where(strict, _mm_nt(kb[:, hsl[h]], kh[h]) * dec[h], 0.0) for h in heads]
    attn = [_mm_nt(qn[:, hsl[h]], kh[h]) * dec[h] for h in heads]
    kdt = [_mm_nt(eye, kdec[:, hsl[h]]) for h in heads]
    xx = [jnp.concatenate([vb[:, hsl[h]], kbe[:, hsl[h]]], axis=1) for h in heads]
    for r in range(6):
        xx = [xx[h] + _mm_inv(p[h], xx[h]) for h in heads]
        if r < 5:
            p = [_mm_inv(p[h], p[h]) for h in heads]
    sh = [s_scr[h] for h in heads]
    v_new = [xx[h][:, :DH] - _mm(xx[h][:, DH:], sh[h]) for h in heads]
    outs = [_mm(qe[:, hsl[h]], sh[h]) + _mm(attn[h], v_new[h]) for h in heads]
    for h in heads:
        s_scr[h] = sh[h] * egl[:, hsl[h]] + _mm(kdt[h], v_new[h])
    o = jnp.concatenate(outs, axis=1)
    on = o * lax.rsqrt(_head_sum(o * o, bd) * (1.0 / DH) + EPS) * ng_ref[...]
    z = z_ref[...]
    o_ref[...] = (on * (z * _sigmoid(z))).astype(BF16)

    @pl.when(c == pl.num_programs(0) - 1)
    def _():
        s_out_ref[...] = s_scr[...]


def _gdn_prompt(qkv, a_e, b_e, a_t, z, conv_w, alog_e, dtb_e, alog_t, dtb_t, ng, bd, ltri, utri, eye):
    t = qkv.shape[0]
    row = lambda w: pl.BlockSpec((CHUNK, w), lambda i: (i, 0))
    full = lambda a: pl.BlockSpec(a.shape, lambda i: (0,) * a.ndim)
    return pl.pallas_call(
        _gdn_body,
        grid=(t // CHUNK,),
        in_specs=[row(CONV_DIM), row(HW), row(HW), pl.BlockSpec((1, HEADS, CHUNK), lambda i: (i, 0, 0)), row(HW),
                  full(conv_w), full(alog_e), full(dtb_e), full(alog_t), full(dtb_t), full(ng), full(bd),
                  full(ltri), full(utri), full(eye)],
        out_specs=[row(HW), pl.BlockSpec((HEADS, DH, DH), lambda i: (0, 0, 0))],
        out_shape=[jax.ShapeDtypeStruct((t, HW), BF16), jax.ShapeDtypeStruct((HEADS, DH, DH), F32)],
        scratch_shapes=[pltpu.VMEM((CHUNK + 8, CONV_DIM), F32), pltpu.VMEM((HEADS, DH, DH), F32)],
        compiler_params=_cparams(("arbitrary",)),
        name="gdn_prompt",
    )(qkv, a_e, b_e, a_t, z, conv_w, alog_e, dtb_e, alog_t, dtb_t, ng, bd, ltri, utri, eye)


def _gdn_step_body(xq_ref, xk_ref, xv_ref, cq_ref, ck_ref, cv_ref, a_ref, b_ref, alog_ref, dtb_ref, z_ref, ng_ref,
                   s0_ref, o_ref, s1_ref):
    def conv(x_ref, c_ref):
        acc = x_ref[0, 0] * c_ref[0, 0]
        for i in range(1, CONV_WIDTH):
            acc = acc + x_ref[i, 0] * c_ref[i, 0]
        return acc * _sigmoid(acc)

    q = conv(xq_ref, cq_ref)
    k = conv(xk_ref, ck_ref)
    v = conv(xv_ref, cv_ref)
    qn = q * lax.rsqrt(jnp.sum(q * q, axis=0, keepdims=True) + EPS) * (DH ** -0.5)
    kn = k * lax.rsqrt(jnp.sum(k * k, axis=0, keepdims=True) + EPS)
    beta = _sigmoid(b_ref[0])
    eg = jnp.exp(-jnp.exp(alog_ref[0]) * _softplus(a_ref[0] + dtb_ref[0]))
    ks = jnp.zeros_like(v)
    for d in range(DH):
        ks = ks + kn[d:d + 1, :] * s0_ref[0, d]
    delta = (v - ks * eg) * beta
    o = jnp.zeros_like(v)
    for d in range(DH):
        s_new = s0_ref[0, d] * eg + kn[d:d + 1, :] * delta
        s1_ref[0, d] = s_new
        o = o + qn[d:d + 1, :] * s_new
    on = o * lax.rsqrt(jnp.mean(o * o, axis=0, keepdims=True) + EPS) * ng_ref[...]
    z = z_ref[0]
    o_ref[0] = on * (z * _sigmoid(z))


def _gdn_step(xq, xk, xv, cq, ck, cv, a_t, b_t, alog_t, dtb_t, z_t, ng_t, s0_t):
    b = xq.shape[-1]
    x_spec = pl.BlockSpec((CONV_WIDTH, 1, DH, b), lambda h: (0, h, 0, 0))
    r_spec = pl.BlockSpec((1, 1, b), lambda h: (h, 0, 0))
    s_spec = pl.BlockSpec((1, DH, DH, b), lambda h: (h, 0, 0, 0))
    hd_spec = pl.BlockSpec((1, DH, b), lambda h: (h, 0, 0))
    return pl.pallas_call(
        _gdn_step_body,
        grid=(HEADS,),
        in_specs=[x_spec, x_spec, x_spec, x_spec, x_spec, x_spec, r_spec, r_spec, r_spec, r_spec, hd_spec,
                  pl.BlockSpec((DH, b), lambda h: (0, 0)), s_spec],
        out_specs=[hd_spec, s_spec],
        out_shape=[jax.ShapeDtypeStruct((HEADS, DH, b), F32), jax.ShapeDtypeStruct((HEADS, DH, DH, b), F32)],
        compiler_params=_cparams(("parallel",)),
        name="gdn_step",
    )(xq, xk, xv, cq, ck, cv, a_t, b_t, alog_t, dtb_t, z_t, ng_t, s0_t)


def _count_tiles(sc, n_tiles, pred):
    def body(kt, cnt):
        hit = jnp.where(pred(kt, sc[kt]), 1, 0).astype(I32)
        return cnt + jnp.sum(hit.reshape(BK // 8, 8, BQ), axis=0)

    cnt = lax.fori_loop(0, n_tiles, body, jnp.zeros((8, BQ), I32))
    return jnp.sum(cnt, axis=0, keepdims=True)


def _count_ge(sc, n_tiles, cand):
    return _count_tiles(sc, n_tiles, lambda kt, s: s >= cand)


def _key_to_f32(key):
    f = pltpu.bitcast(jnp.where(key < 0, key ^ jnp.int32(0x7FFFFFFF), key), F32)
    return jnp.where(key <= INT_MIN + 0x7FFFFF, -jnp.inf, f)


def _greedy_key(count_fn, k, shape, bits):
    lo = jnp.where(count_fn(jnp.zeros(shape, F32)) >= k, 0, INT_MIN).astype(I32)

    def bit(i, lo):
        cand = lo + jnp.left_shift(jnp.int32(1), 30 - i)
        return jnp.where(count_fn(_key_to_f32(cand)) >= k, cand, lo)

    return lax.fori_loop(0, bits, bit, lo)


def _kth_largest(count_fn, k, shape):
    return _key_to_f32(_greedy_key(count_fn, k, shape, 31))


def _kth_largest_bracketed(count_fn, k, lo, hi):
    def unresolved(lo, hi, c_lo):
        gap = hi - lo
        return ((gap > 1) | (gap < 0)) & (c_lo != k)

    def cond(st):
        i, lo, hi, c_lo = st
        return (i < 34) & (jnp.max(jnp.where(unresolved(lo, hi, c_lo), 1, 0)) > 0)

    def body(st):
        i, lo, hi, c_lo = st
        for _ in range(BISECT_STEPS_PER_CHECK):
            mid = lo + lax.shift_right_logical(hi - lo, 1)
            c = count_fn(_key_to_f32(mid))
            take = c >= k
            lo, hi, c_lo = jnp.where(take, mid, lo), jnp.where(take, hi, mid), jnp.where(take, c, c_lo)
        return i + BISECT_STEPS_PER_CHECK, lo, hi, c_lo

    _, lo, _, c_lo = lax.while_loop(cond, body, (jnp.int32(0), lo, hi, count_fn(_key_to_f32(lo))))
    return lo, c_lo


def _dsa_body(kib_ref, qit_ref, wt_ref, qt_ref, k_ref, vt_ref, o_ref, sc, gm, thr, m_s, a_s, lg_s, acc, qz, *,
              topk, idx_bits):
    qb = pl.program_id(0)
    kb = pl.program_id(1)
    q0 = qb * BQ
    n_tiles = (q0 + BQ + BK - 1) // BK
    qpos = q0 + lax.broadcasted_iota(I32, (BK, BQ), 1)
    rowi = lax.broadcasted_iota(I32, (BK, BQ), 0)

    @pl.when(kb == 0)
    def _score():
        w = wt_ref[...]

        def tile(kt, carry):
            ki_t = kib_ref[kt]
            s = jnp.zeros((BK, BQ), F32)
            for h in range(HEADS):
                sh = jnp.dot(ki_t, qit_ref[DH * h:DH * (h + 1), :], preferred_element_type=F32)
                s = s + jnp.maximum(sh, 0.0) * w[h:h + 1, :]
            vis = (kt * BK + rowi) <= qpos
            s = jnp.where(vis, s, -jnp.inf)
            sc[kt] = s
            gm[pl.ds(pl.multiple_of(kt * 8, 8), 8), :] = jnp.max(s.reshape(BK // 8, 8, BQ), axis=0)
            return carry

        gm[...] = jnp.full_like(gm, -jnp.inf)
        lax.fori_loop(0, n_tiles, tile, 0)

        def gm_count(cand):
            hit = jnp.where(gm[...] >= cand, 1, 0).astype(I32)
            return jnp.sum(jnp.sum(hit.reshape(gm.shape[0] // 8, 8, BQ), axis=0), axis=0, keepdims=True)

        k_up = max(topk // (BK // 8), 1)
        lo = _greedy_key(gm_count, topk, (1, BQ), BRACKET_BITS)
        hi = _greedy_key(gm_count, k_up, (1, BQ), BRACKET_BITS) + (1 << (31 - BRACKET_BITS))
        key, c_key = _kth_largest_bracketed(functools.partial(_count_ge, sc, n_tiles), topk, lo, hi)
        t = _key_to_f32(key)
        thr[...] = t
        over = (c_key > topk) & (t > -jnp.inf)

        @pl.when(jnp.max(jnp.where(over, 1, 0)) > 0)
        def _break_ties():
            room = topk - _count_tiles(sc, n_tiles, lambda kt, s: s > t)

            def bit(i, x):
                cand = x + jnp.left_shift(jnp.int32(1), idx_bits - 1 - i)
                below = _count_tiles(sc, n_tiles, lambda kt, s: (s == t) & ((kt * BK + rowi) < cand))
                return jnp.where(below < room, cand, x)

            last_kept = lax.fori_loop(0, idx_bits, bit, jnp.zeros((1, BQ), I32))

            def drop(kt, carry):
                s = sc[kt]
                sc[kt] = jnp.where((s == t) & ((kt * BK + rowi) > last_kept), -jnp.inf, s)
                return carry

            lax.fori_loop(0, n_tiles, drop, 0)

        m_s[...] = jnp.full_like(m_s, NEG)
        acc[...] = jnp.zeros_like(acc)
        zero = jnp.zeros((DH, BQ), BF16)
        for h in range(HEADS):
            qh = qt_ref[DH * h:DH * (h + 1), :]
            qz[h, 0:DH, :] = qh if h % 2 == 0 else zero
            qz[h, DH:2 * DH, :] = zero if h % 2 == 0 else qh

    @pl.when(kb * (KV_TILES_PER_BLOCK * BK) < q0 + BQ)
    def _attend():
        t0 = kb * KV_TILES_PER_BLOCK
        th = thr[...]

        def tile(j, carry):
            gt = t0 + j
            msk = (sc[gt] >= th) & ((gt * BK + rowi) <= qpos)
            bias = jnp.where(msk, 0.0, NEG)
            for h in range(HEADS):
                pair = LANES * (h // 2)
                lg = jnp.dot(k_ref[j, :, pair:pair + LANES], qz[h], preferred_element_type=F32) + bias
                lg_s[h] = lg
                m_old = m_s[h]
                m_new = jnp.maximum(m_old, jnp.max(lg, axis=0, keepdims=True))
                a_s[h] = jnp.exp2(m_old - m_new)
                m_s[h] = m_new
            for h in range(HEADS):
                p = jnp.exp2(lg_s[h] - m_s[h])
                pv = jnp.dot(vt_ref[j, h], p.astype(BF16), preferred_element_type=F32)
                acc[h] = a_s[h] * acc[h] + pv
            return carry

        lax.fori_loop(0, jnp.minimum(KV_TILES_PER_BLOCK, n_tiles - t0), tile, 0)

    @pl.when(kb == pl.num_programs(1) - 1)
    def _finish():
        for h in range(HEADS):
            o_ref[DH * h:DH * (h + 1), :] = acc[h, 0:DH, :] / acc[h, DH:DH + 1, :]


def _dsa_prompt(kib3, qit, wt, qt, k3, vt3, topk):
    nkt = kib3.shape[0]
    t = nkt * BK
    nq = t // BQ
    nkb = -(-nkt // KV_TILES_PER_BLOCK)
    kvb = min(KV_TILES_PER_BLOCK, nkt)
    span = kvb * BK

    def kv_idx(qb, kb):
        return (jnp.minimum(kb, (qb * BQ + BQ - 1) // span), 0, 0)

    return pl.pallas_call(
        functools.partial(_dsa_body, topk=topk, idx_bits=(t - 1).bit_length()),
        grid=(nq, nkb),
        in_specs=[pl.BlockSpec((nkt, BK, DH), lambda qb, kb: (0, 0, 0)),
                  pl.BlockSpec((HW, BQ), lambda qb, kb: (0, qb)),
                  pl.BlockSpec((HEADS, BQ), lambda qb, kb: (0, qb)),
                  pl.BlockSpec((HW, BQ), lambda qb, kb: (0, qb)),
                  pl.BlockSpec((kvb, BK, HW), kv_idx),
                  pl.BlockSpec((kvb, HEADS, V_ROWS, BK), lambda qb, kb: kv_idx(qb, kb) + (0,))],
        out_specs=pl.BlockSpec((HW, BQ), lambda qb, kb: (0, qb)),
        out_shape=jax.ShapeDtypeStruct((HW, t), F32),
        scratch_shapes=[pltpu.VMEM((nkt, BK, BQ), F32), pltpu.VMEM((nkt * 8, BQ), F32), pltpu.VMEM((1, BQ), F32),
                        pltpu.VMEM((HEADS, 1, BQ), F32), pltpu.VMEM((HEADS, 1, BQ), F32),
                        pltpu.VMEM((HEADS, BK, BQ), F32), pltpu.VMEM((HEADS, V_ROWS, BQ), F32),
                        pltpu.VMEM((HEADS, 2 * DH, BQ), BF16)],
        compiler_params=_cparams(("parallel", "arbitrary")),
        name="dsa_prompt",
    )(kib3, qit, wt, qt, k3, vt3)


def _dsa_scores_body(pt_ref, *refs, page):
    del pt_ref
    pp = PAGES_PER_STEP
    ik_refs = refs[0:pp]
    qi_ref, w_ref, sc_ref = refs[pp:]
    pages = jnp.concatenate([r[...] for r in ik_refs], axis=1)
    s8 = _mm3(qi_ref[...], pages)
    srow = jnp.sum(jnp.maximum(s8, 0.0) * w_ref[...][:, 0:1], axis=0, keepdims=True)
    sc_ref[...] = jnp.concatenate([srow[:, i * page:(i + 1) * page] for i in range(pp)], axis=0)


def _dsa_scores(page_table, cik_t, qi_s, w_s):
    b, n_pages = page_table.shape
    page = cik_t.shape[2]
    pp = PAGES_PER_STEP

    def ik_spec(i):
        return pl.BlockSpec((None, DH, page), lambda bb, c, pt: (pt[bb, c * pp + i], 0, 0))

    per_b = lambda r, w: pl.BlockSpec((None, r, w), lambda bb, c, pt: (bb, 0, 0))
    grid_spec = pltpu.PrefetchScalarGridSpec(
        num_scalar_prefetch=1,
        grid=(b, n_pages // pp),
        in_specs=[ik_spec(i) for i in range(pp)] + [per_b(HEADS, DH), per_b(HEADS, page)],
        out_specs=pl.BlockSpec((None, pp, page), lambda bb, c, pt: (bb, c, 0)),
    )
    return pl.pallas_call(
        functools.partial(_dsa_scores_body, page=page),
        grid_spec=grid_spec,
        out_shape=jax.ShapeDtypeStruct((b, n_pages, page), F32),
        compiler_params=_cparams(("parallel", "arbitrary")),
        name="dsa_scores",
    )(page_table, *([cik_t] * pp), qi_s, w_s)


def _dsa_pick_body(sc_ref, qi_ref, w_ref, kin_ref, mask_ref, new_ref, *, topk):
    s_all = sc_ref[...]
    nb, n_pages, page = s_all.shape
    w = w_ref[...][:, :, 0:1]
    s_new = jnp.sum(qi_ref[...] * kin_ref[...], axis=2, keepdims=True)
    s_new = jnp.sum(jnp.maximum(s_new, 0.0) * w, axis=1, keepdims=True)
    total = lambda m: jnp.sum(jnp.sum(m, axis=2, keepdims=True), axis=1, keepdims=True)

    def count(cand):
        return total(jnp.where(s_all >= cand, 1.0, 0.0)) + jnp.where(s_new >= cand, 1.0, 0.0)

    thr = _kth_largest(count, topk, (nb, 1, 1))
    ri = lax.broadcasted_iota(I32, (page, page), 0)
    ci = lax.broadcasted_iota(I32, (page, page), 1)
    upper = jnp.where(ri <= ci, 1.0, 0.0).astype(BF16)
    rp = lax.broadcasted_iota(I32, (n_pages, n_pages), 0)
    cp = lax.broadcasted_iota(I32, (n_pages, n_pages), 1)
    lower_strict = jnp.where(cp < rp, 1.0, 0.0).astype(BF16)
    gt = s_all > thr
    eq = s_all == thr
    eqf = jnp.where(eq, 1.0, 0.0)
    within = jnp.dot(eqf.reshape(nb * n_pages, page).astype(BF16), upper,
                     preferred_element_type=F32).reshape(nb, n_pages, page)
    tot = jnp.broadcast_to(within[:, :, page - 1:page], within.shape).astype(BF16)
    before = jnp.dot(lower_strict, jnp.concatenate([tot[i] for i in range(nb)], axis=1),
                     preferred_element_type=F32)
    cum_eq = within + jnp.stack([before[:, i * page:(i + 1) * page] for i in range(nb)], axis=0)
    room = topk - total(jnp.where(gt, 1.0, 0.0)) - jnp.where(s_new > thr, 1.0, 0.0)
    sel = gt | (eq & (cum_eq <= room))
    new_sel = (s_new > thr) | ((s_new == thr) & (total(eqf) < room))
    mask_ref[...] = jnp.where(sel, 1.0, 0.0)
    new_ref[...] = jnp.where(new_sel, 1.0, 0.0) * jnp.ones(new_ref.shape, F32)


def _dsa_pick(scores, qi_s, w_s, ki_new, topk):
    b, n_pages, page = scores.shape
    full = lambda a: pl.BlockSpec(a.shape, lambda i: (0,) * a.ndim)
    return pl.pallas_call(
        functools.partial(_dsa_pick_body, topk=topk),
        grid=(1,),
        in_specs=[full(scores), full(qi_s), full(w_s), full(ki_new)],
        out_specs=[pl.BlockSpec((b, n_pages, page), lambda i: (0, 0, 0)),
                   pl.BlockSpec((b, 8, page), lambda i: (0, 0, 0))],
        out_shape=[jax.ShapeDtypeStruct((b, n_pages, page), F32), jax.ShapeDtypeStruct((b, 8, page), F32)],
        compiler_params=_cparams(("arbitrary",)),
        name="dsa_pick",
    )(scores, qi_s, w_s, ki_new)


def _dsa_attend_body(pt_ref, *refs, n_chunks):
    del pt_ref
    pp = PAGES_PER_STEP
    k_refs = refs[0:pp]
    v_refs = refs[pp:2 * pp]
    mask_ref, new_ref, q_ref, kn_ref, vn_ref, o_ref, qb, m_s, l_s, acc = refs[2 * pp:]
    c = pl.program_id(1)

    @pl.when(c == 0)
    def _init():
        qb[...] = jnp.broadcast_to(q_ref[...], qb.shape)
        m_s[...] = jnp.full_like(m_s, NEG)
        l_s[...] = jnp.zeros_like(l_s)
        acc[...] = jnp.zeros_like(acc)

    for i in range(pp):
        sel_row = mask_ref[i:i + 1, :] > 0.0
        for h in range(HEADS):
            lg = jnp.sum(k_refs[i][h] * qb[h], axis=0, keepdims=True)
            lg = jnp.where(sel_row, lg, NEG)
            m_old = m_s[h]
            m_new = jnp.maximum(m_old, jnp.max(lg, axis=1, keepdims=True))
            alpha = jnp.exp(m_old - m_new)
            p = jnp.exp(lg - m_new)
            l_s[h] = alpha * l_s[h] + p
            acc[h] = alpha * acc[h] + p * v_refs[i][h]
            m_s[h] = m_new

    @pl.when(c == n_chunks - 1)
    def _finish():
        new_sel = new_ref[0:1, 0:1].reshape(1, 1, 1) > 0.0
        lg_new = jnp.where(new_sel, jnp.sum(q_ref[...] * kn_ref[...], axis=1, keepdims=True), NEG)
        m_old = m_s[...][:, :, 0:1]
        m_fin = jnp.maximum(m_old, lg_new)
        scale = jnp.exp(m_old - m_fin)
        p_new = jnp.exp(lg_new - m_fin)
        den = jnp.sum(l_s[...], axis=2, keepdims=True) * scale + p_new
        o_ref[...] = (jnp.sum(acc[...], axis=2, keepdims=True) * scale + p_new * vn_ref[...]) / den


def _dsa_attend(page_table, ck_t, cv_t, mask, new_sel, q_c, k_c, v_c):
    b, n_pages = page_table.shape
    page = ck_t.shape[3]
    pp = PAGES_PER_STEP
    n_chunks = n_pages // pp

    def kv_spec(i):
        return pl.BlockSpec((None, HEADS, DH, page), lambda bb, c, pt: (pt[bb, c * pp + i], 0, 0, 0))

    col = pl.BlockSpec((None, HEADS, DH, 1), lambda bb, c, pt: (bb, 0, 0, 0))
    grid_spec = pltpu.PrefetchScalarGridSpec(
        num_scalar_prefetch=1,
        grid=(b, n_chunks),
        in_specs=([kv_spec(i) for i in range(pp)] + [kv_spec(i) for i in range(pp)]
                  + [pl.BlockSpec((None, pp, page), lambda bb, c, pt: (bb, c, 0)),
                     pl.BlockSpec((None, 8, page), lambda bb, c, pt: (bb, 0, 0)), col, col, col]),
        out_specs=col,
        scratch_shapes=[pltpu.VMEM((HEADS, DH, page), F32), pltpu.VMEM((HEADS, 1, page), F32),
                        pltpu.VMEM((HEADS, 1, page), F32), pltpu.VMEM((HEADS, DH, page), F32)],
    )
    return pl.pallas_call(
        functools.partial(_dsa_attend_body, n_chunks=n_chunks),
        grid_spec=grid_spec,
        out_shape=jax.ShapeDtypeStruct((b, HEADS, DH, 1), F32),
        compiler_params=_cparams(("parallel", "arbitrary")),
        name="dsa_attend",
    )(page_table, *([ck_t] * pp), *([cv_t] * pp), mask, new_sel, q_c, k_c, v_c)


def _route(logits):
    lane = lax.broadcasted_iota(I32, logits.shape, 1)
    mx = lambda a: jnp.max(a, axis=-1, keepdims=True)
    sm = lambda a: jnp.sum(a, axis=-1, keepdims=True)
    first = lambda hit: jnp.min(jnp.where(hit, lane, LANES), axis=-1, keepdims=True)
    gmask = lane < N_GROUPS
    lg = jnp.where(gmask, logits, NEG)
    gex = jnp.where(gmask, jnp.exp(lg - mx(lg)), 0.0)
    gp = gex / sm(gex)
    g_prob = mx(gp)
    g_idx = first(gmask & (gp == g_prob))
    e_lo = N_GROUPS + EPG * g_idx
    emask = (lane >= e_lo) & (lane < e_lo + EPG)
    le = jnp.where(emask, logits, NEG)
    eex = jnp.where(emask, jnp.exp(le - mx(le)), 0.0)
    ep = jnp.where(emask, eex / sm(eex), -1.0)
    p1 = mx(ep)
    i1 = first(ep == p1)
    ep2 = jnp.where(lane == i1, -1.0, ep)
    p2 = mx(ep2)
    i2 = first(ep2 == p2)
    den = p1 + p2
    return jnp.where(lane == i1, g_prob * p1 / den, 0.0) + jnp.where(lane == i2, g_prob * p2 / den, 0.0)


def _finish_body(x_ref, mix_ref, wo_ref, g_ref, wr1_ref, wr2_ref, wr3_ref, wg_ref, wu_ref, wd_ref,
                 y_ref, h2b, cw, acc, *, exact):
    e = pl.program_id(1)

    @pl.when(e == 0)
    def _():
        if exact:
            proj = jnp.dot(mix_ref[...], wo_ref[...], precision=lax.Precision.HIGHEST, preferred_element_type=F32)
        else:
            proj = jnp.dot(mix_ref[...], wo_ref[...], preferred_element_type=F32)
        y1 = x_ref[...] + proj
        ms = jnp.mean(y1 * y1, axis=-1, keepdims=True)
        h2 = y1 * lax.rsqrt(ms + EPS) * g_ref[...]
        h2b[...] = h2.astype(BF16)
        acc[...] = y1
        a1, a2, a3 = _split3(h2)
        d = lambda a, b_ref: jnp.dot(a, b_ref[...], preferred_element_type=F32)
        logits = (d(a1, wr1_ref) + d(a1, wr2_ref) + d(a2, wr1_ref)
                  + d(a1, wr3_ref) + d(a2, wr2_ref) + d(a3, wr1_ref))
        cw[...] = _route(logits)

    hb = h2b[...]
    cwv = cw[...]
    lane = lax.broadcasted_iota(I32, cwv.shape, 1)
    for j in range(EXPERTS_PER_STEP):
        eid = e * EXPERTS_PER_STEP + j
        gte = jnp.dot(hb, wg_ref[j], preferred_element_type=F32)
        up = jnp.dot(hb, wu_ref[j], preferred_element_type=F32)
        cwe = jnp.sum(jnp.where(lane == N_GROUPS + eid, cwv, 0.0), axis=-1, keepdims=True)
        a = (gte * _sigmoid(gte)) * up * cwe
        acc[...] += jnp.dot(a.astype(BF16), wd_ref[j], preferred_element_type=F32)

    @pl.when(e == pl.num_programs(1) - 1)
    def _():
        y_ref[...] = acc[...]


def _finish(x_all, mix, wo, ffn_g, wr1, wr2, wr3, wg, wu, wd, tm, exact):
    tp = x_all.shape[0]
    eb = EXPERTS_PER_STEP
    row = lambda w: pl.BlockSpec((tm, w), lambda i, e: (i, 0))
    full = lambda a: pl.BlockSpec(a.shape, lambda i, e: (0,) * a.ndim)
    return pl.pallas_call(
        functools.partial(_finish_body, exact=exact),
        grid=(tp // tm, N_EXPERTS // eb),
        in_specs=[row(D_MODEL), row(2 * HW), full(wo), full(ffn_g), full(wr1), full(wr2), full(wr3),
                  pl.BlockSpec((eb, D_MODEL, D_EXPERT), lambda i, e: (e, 0, 0)),
                  pl.BlockSpec((eb, D_MODEL, D_EXPERT), lambda i, e: (e, 0, 0)),
                  pl.BlockSpec((eb, D_EXPERT, D_MODEL), lambda i, e: (e, 0, 0))],
        out_specs=row(D_MODEL),
        out_shape=jax.ShapeDtypeStruct((tp, D_MODEL), F32),
        scratch_shapes=[pltpu.VMEM((tm, D_MODEL), BF16), pltpu.VMEM((tm, LANES), F32),
                        pltpu.VMEM((tm, D_MODEL), F32)],
        compiler_params=_cparams(("parallel", "arbitrary")),
        name="finish_exact" if exact else "finish",
    )(x_all, mix, wo, ffn_g, wr1, wr2, wr3, wg, wu, wd)


def _tile_heads(g):
    return jnp.tile(g.astype(F32), HEADS)[None, :]


def kernel(x_prompt, x_sample, cache_k, cache_v, cache_idx_k, state_gdn, state_conv, page_table, attn_norm_g, w_in,
           conv_w, A_log, dt_bias, gdn_norm_g, q_norm_g, k_norm_g, idx_k_norm_g, w_out, ffn_norm_g, w_router_group,
           w_router_expert, w_gate, w_up, w_down):
    assert w_in.shape[0] == 1 and x_prompt.shape[0] == 1 and x_sample.shape[1] == 1
    t = x_prompt.shape[1]
    nb = x_sample.shape[0]
    assert t % BQ == 0 and t % CHUNK == 0 and BQ == BK and t % TM_FIN == 0

    offs = np.concatenate([[0], np.cumsum(IN_SPLITS)])
    seg = lambda i: w_in[0][:, offs[i]:offs[i + 1]]
    tail_pad = LANES - (DH + 3 * HEADS)
    w_r = jnp.concatenate([seg(0), seg(1), seg(4), seg(5), seg(6), seg(7), seg(8), seg(2), seg(3), seg(9),
                           jnp.zeros((D_MODEL, tail_pad), F32)], axis=1).astype(F32)
    hid = np.arange(HW) // DH
    bd = jnp.asarray(hid[:, None] == hid[None, :], BF16)
    tri = np.arange(CHUNK)
    ltri = jnp.asarray(tri[:, None] >= tri[None, :], BF16)
    utri = jnp.asarray(tri[:, None] <= tri[None, :], BF16)
    eye = jnp.asarray(tri[:, None] == tri[None, :], BF16)

    norm_args = (attn_norm_g.astype(F32), bd, _tile_heads(q_norm_g[0]), _tile_heads(k_norm_g[0]),
                 idx_k_norm_g.astype(F32))
    x_p = x_prompt[0].astype(F32)
    x_s = x_sample[:, 0].astype(F32)
    (qkv, z, q_bf, k_f, k_bf, v_f, v_bf, qi_bf, ki_f, misc) = _inproj(
        x_p, norm_args[0], w_r.astype(BF16), *norm_args[1:], tm=TM_IN, exact=False)
    (qkv_s, z_s, q_s, k_s, _, v_s, _, qi_s, ki_s, misc_s) = _inproj(
        x_s, norm_args[0], w_r, *norm_args[1:], tm=nb, exact=True)
    split_misc = lambda m: (m[:, DH:DH + HEADS], m[:, DH + HEADS:DH + 2 * HEADS],
                            m[:, DH + 2 * HEADS:DH + 3 * HEADS] * ((HEADS * DH) ** -0.5))
    a_p, b_p, wi_p = split_misc(misc)
    a_s, b_s, wi_s = split_misc(misc_s)

    rep = lambda a: jnp.repeat(a, DH, axis=-1)
    alog = A_log[0].astype(F32)
    dtb = dt_bias[0].astype(F32)
    n_chunks = t // CHUNK
    o_g_p, s_p = _gdn_prompt(
        qkv, rep(a_p), rep(b_p), a_p.reshape(n_chunks, CHUNK, HEADS).swapaxes(1, 2), z,
        conv_w[0].astype(F32), rep(alog)[None, :], rep(dtb)[None, :],
        jnp.broadcast_to(alog[:, None], (HEADS, CHUNK)), jnp.broadcast_to(dtb[:, None], (HEADS, CHUNK)),
        _tile_heads(gdn_norm_g[0]), bd, ltri, utri, eye)
    conv_p = qkv[t - (CONV_WIDTH - 1):t]

    ext = jnp.concatenate([state_conv[0].astype(F32), qkv_s[:, None, :]], axis=1)
    ext_t = ext.transpose(1, 2, 0).reshape(CONV_WIDTH, 3, HEADS, DH, nb)
    cw_t = jnp.broadcast_to(conv_w[0].astype(F32).reshape(CONV_WIDTH, 3, HEADS, DH, 1), ext_t.shape)
    row_t = lambda a: jnp.broadcast_to(a, (HEADS, nb)).reshape(HEADS, 1, nb)
    o_g_st, s_st = _gdn_step(
        ext_t[:, 0], ext_t[:, 1], ext_t[:, 2], cw_t[:, 0], cw_t[:, 1], cw_t[:, 2],
        row_t(a_s.T), row_t(b_s.T), row_t(alog[:, None]), row_t(dtb[:, None]),
        z_s.T.reshape(HEADS, DH, nb), jnp.broadcast_to(gdn_norm_g[0].astype(F32)[:, None], (DH, nb)),
        state_gdn[0].astype(F32).transpose(1, 2, 3, 0))
    o_g_s = o_g_st.reshape(HW, nb).T
    s_s = s_st.transpose(3, 0, 1, 2)
    conv_s = ext[:, 1:]

    nkt = t // BK
    vt_aug = jnp.concatenate([v_bf.reshape(nkt, BK, HEADS, DH).transpose(0, 2, 3, 1),
                              jnp.ones((nkt, HEADS, V_ROWS - DH, BK), BF16)], axis=2)
    o_a_t = _dsa_prompt(
        ki_f.astype(BF16).reshape(nkt, BK, DH), qi_bf.T, wi_p.T, q_bf.T,
        k_bf.reshape(nkt, BK, HW), vt_aug, min(TOPK_MAX, t // 4))
    o_a_p = o_a_t.T

    n_pool, page = cache_idx_k.shape[1], cache_idx_k.shape[2]
    past = page_table.shape[1] * page
    pt = page_table.astype(I32)
    qi_h = qi_s.reshape(nb, HEADS, DH)
    wi_b = jnp.broadcast_to(wi_s[:, :, None], (nb, HEADS, page))
    scores = _dsa_scores(pt, cache_idx_k[0].astype(F32).transpose(0, 2, 1), qi_h, wi_b)
    mask, new_sel = _dsa_pick(scores, qi_h, wi_b, ki_s[:, None, :], min(TOPK_MAX, (past + 1) // 4))
    col = lambda a: a.reshape(nb, HEADS, DH, 1)
    o_a_s = _dsa_attend(
        pt, cache_k[0].astype(F32).transpose(0, 2, 3, 1), cache_v[0].astype(F32).transpose(0, 2, 3, 1),
        mask, new_sel, col(q_s), col(k_s), col(v_s)).reshape(nb, HW)

    mix_p = jnp.concatenate([o_g_p, o_a_p.astype(BF16)], axis=1)
    mix_s = jnp.concatenate([o_g_s, o_a_s], axis=1)
    w_router = jnp.concatenate([
        w_router_group[0], w_router_expert[0].transpose(1, 0, 2).reshape(D_MODEL, N_EXPERTS),
        jnp.zeros((D_MODEL, LANES - N_GROUPS - N_EXPERTS), F32)], axis=1).astype(F32)
    wr1 = w_router.astype(BF16)
    wr2 = (w_router - wr1.astype(F32)).astype(BF16)
    wr3 = (w_router - wr1.astype(F32) - wr2.astype(F32)).astype(BF16)
    experts = (w_gate[0].astype(BF16), w_up[0].astype(BF16), w_down[0].astype(BF16))
    y_p = _finish(x_p, mix_p, w_out[0].astype(BF16), ffn_norm_g.astype(F32), wr1, wr2, wr3, *experts,
                  tm=TM_FIN, exact=False)
    y_s = _finish(x_s, mix_s, w_out[0].astype(F32), ffn_norm_g.astype(F32), wr1, wr2, wr3, *experts,
                  tm=nb, exact=True)

    return (y_p[None], y_s[:, None, :],
            k_f.reshape(1, 1, t, HEADS, DH), v_f.reshape(1, 1, t, HEADS, DH), ki_f[None, None],
            s_p[None, None], conv_p[None, None],
            k_s.reshape(1, nb, 1, HEADS, DH), v_s.reshape(1, nb, 1, HEADS, DH),
            ki_s.reshape(1, nb, 1, DH), s_s[None], conv_s[None])
```

```python
import functools

import jax
import jax.numpy as jnp
import numpy as np
from jax import lax
from jax.experimental import pallas as pl
from jax.experimental.pallas import tpu as pltpu

F32 = jnp.float32
BF16 = jnp.bfloat16
I32 = jnp.int32
EPS = 1e-6
NEG = -1e30
INT_MIN = -(2 ** 31)

D_MODEL = 1024
HEADS = 8
DH = 64
HW = HEADS * DH
CONV_DIM = 3 * HW
CONV_WIDTH = 4
CHUNK = 64
TOPK_MAX = 256
N_GROUPS = 4
EPG = 8
N_EXPERTS = N_GROUPS * EPG
D_EXPERT = 256
LANES = 128
IN_SPLITS = (CONV_DIM, HW, HEADS, HEADS, HW, HW, HW, HW, DH, HEADS)

TM_IN = 256
TM_FIN = 512
EXPERTS_PER_STEP = 4
BQ = 256
BK = 256
KV_TILES_PER_BLOCK = 8
V_ROWS = DH + 16
BRACKET_BITS = 13
BISECT_STEPS_PER_CHECK = 4
LOG2E = 1.4426950408889634
PAGES_PER_STEP = 16
VMEM_LIMIT = 52 * 1024 * 1024


def _cparams(sem):
    return pltpu.CompilerParams(dimension_semantics=sem, vmem_limit_bytes=VMEM_LIMIT)


def _sigmoid(x):
    return 1.0 / (1.0 + jnp.exp(-x))


def _softplus(x):
    return jnp.maximum(x, 0.0) + jnp.log(1.0 + jnp.exp(-jnp.abs(x)))


def _mm(a, b):
    return jnp.dot(a.astype(BF16), b.astype(BF16), preferred_element_type=F32)


def _mm_nt(a, b):
    return lax.dot_general(a.astype(BF16), b.astype(BF16), (((1,), (1,)), ((), ())),
                           preferred_element_type=F32)


def _split2(x):
    hi = x.astype(BF16)
    return hi, (x - hi.astype(F32)).astype(BF16)


def _mm3(a, b):
    ah, al = _split2(a)
    bh, bl = _split2(b)
    d = lambda x, y: jnp.dot(x, y, preferred_element_type=F32)
    return d(ah, bh) + d(ah, bl) + d(al, bh)


_mm_inv = _mm3


def _mm_nt3(a, b):
    ah, al = _split2(a)
    bh, bl = _split2(b)
    d = lambda x, y: lax.dot_general(x, y, (((1,), (1,)), ((), ())), preferred_element_type=F32)
    return d(ah, bh) + d(ah, bl) + d(al, bh)


def _split3(x):
    x1 = x.astype(BF16)
    r = x - x1.astype(F32)
    x2 = r.astype(BF16)
    x3 = (r - x2.astype(F32)).astype(BF16)
    return x1, x2, x3


def _dot_ones_l(ones_bf, x):
    x1, x2, x3 = _split3(x)
    d = lambda p: jnp.dot(ones_bf, p, preferred_element_type=F32)
    return d(x1) + d(x2) + d(x3)


def _dot_ones_r(x, ones_bf):
    x1, x2, x3 = _split3(x)
    d = lambda p: jnp.dot(p, ones_bf, preferred_element_type=F32)
    return d(x1) + d(x2) + d(x3)


def _head_sum(y, bd):
    return _dot_ones_r(y, bd)


def _inproj_body(x_ref, g_ref, w_ref, bd_ref, qg_ref, kg_ref, ikg_ref,
                 qkv_ref, z_ref, q_ref, k_ref, kb_ref, v_ref, vb_ref, qi_ref, ki_ref, misc_ref, *, exact):
    x = x_ref[...]
    ms = jnp.mean(x * x, axis=-1, keepdims=True)
    h = x * lax.rsqrt(ms + EPS) * g_ref[...]
    if not exact:
        h = h.astype(BF16)

    def seg(lo, n):
        if exact:
            return jnp.dot(h, w_ref[:, lo:lo + n], precision=lax.Precision.HIGHEST, preferred_element_type=F32)
        return jnp.dot(h, w_ref[:, lo:lo + n], preferred_element_type=F32)

    for j in range(3):
        qkv_ref[:, j * HW:(j + 1) * HW] = seg(j * HW, HW)
    z_ref[...] = seg(3 * HW, HW)
    bd = bd_ref[...]

    def head_rms(y, gain):
        return y * lax.rsqrt(_head_sum(y * y, bd) * (1.0 / DH) + EPS) * gain

    q = head_rms(seg(4 * HW, HW), qg_ref[...])
    q_ref[...] = (q * (DH ** -0.5 * (1.0 if exact else LOG2E))).astype(q_ref.dtype)
    k = head_rms(seg(5 * HW, HW), kg_ref[...])
    k_ref[...] = k
    kb_ref[...] = k.astype(BF16)
    v = seg(6 * HW, HW)
    v_ref[...] = v
    vb_ref[...] = v.astype(BF16)
    qi_ref[...] = seg(7 * HW, HW).astype(qi_ref.dtype)
    tail = seg(8 * HW, LANES)
    ki = tail[:, :DH]
    ki_ms = jnp.mean(ki * ki, axis=-1, keepdims=True)
    ki_ref[...] = ki * lax.rsqrt(ki_ms + EPS) * ikg_ref[...]
    misc_ref[...] = tail


def _inproj(x_all, attn_g, w_r, bd, qg, kg, ikg, tm, exact):
    tp = x_all.shape[0]
    nw = w_r.shape[1]
    qdt = F32 if exact else BF16
    row = lambda w: pl.BlockSpec((tm, w), lambda i: (i, 0))
    full = lambda a: pl.BlockSpec(a.shape, lambda i: (0,) * a.ndim)
    out_shapes = [
        jax.ShapeDtypeStruct((tp, CONV_DIM), F32),
        jax.ShapeDtypeStruct((tp, HW), F32),
        jax.ShapeDtypeStruct((tp, HW), qdt),
        jax.ShapeDtypeStruct((tp, HW), F32),
        jax.ShapeDtypeStruct((tp, HW), BF16),
        jax.ShapeDtypeStruct((tp, HW), F32),
        jax.ShapeDtypeStruct((tp, HW), BF16),
        jax.ShapeDtypeStruct((tp, HW), qdt),
        jax.ShapeDtypeStruct((tp, DH), F32),
        jax.ShapeDtypeStruct((tp, LANES), F32),
    ]
    out_specs = [row(CONV_DIM), row(HW), row(HW), row(HW), row(HW), row(HW), row(HW), row(HW),
                 row(DH), row(LANES)]
    return pl.pallas_call(
        functools.partial(_inproj_body, exact=exact),
        grid=(tp // tm,),
        in_specs=[row(D_MODEL), full(attn_g), pl.BlockSpec((D_MODEL, nw), lambda i: (0, 0)), full(bd),
                  full(qg), full(kg), full(ikg)],
        out_specs=out_specs,
        out_shape=out_shapes,
        compiler_params=_cparams(("parallel",)),
        name="inproj_exact" if exact else "inproj",
    )(x_all, attn_g, w_r, bd, qg, kg, ikg)


def _gdn_body(qkv_ref, ae_ref, be_ref, at_ref, z_ref, cw_ref, alog_ref, dtb_ref, alogt_ref, dtbt_ref,
              ng_ref, bd_ref, ltri_ref, utri_ref, eye_ref,
              o_ref, s_out_ref, xb, s_scr):
    c = pl.program_id(0)

    @pl.when(c == 0)
    def _():
        xb[0:8, :] = jnp.zeros((8, CONV_DIM), F32)
        s_scr[...] = jnp.zeros_like(s_scr)

    xb[8:8 + CHUNK, :] = qkv_ref[...]
    base = 8 - (CONV_WIDTH - 1)
    conv = xb[base:base + CHUNK, :] * cw_ref[0:1, :]
    for i in range(1, CONV_WIDTH):
        conv = conv + xb[base + i:base + i + CHUNK, :] * cw_ref[i:i + 1, :]
    xb[0:8, :] = xb[CHUNK:CHUNK + 8, :]
    act = conv * _sigmoid(conv)
    q = act[:, 0:HW]
    k = act[:, HW:2 * HW]
    v = act[:, 2 * HW:3 * HW]
    bd = bd_ref[...]
    qn = q * lax.rsqrt(_head_sum(q * q, bd) + EPS) * (DH ** -0.5)
    kn = k * lax.rsqrt(_head_sum(k * k, bd) + EPS)
    beta = _sigmoid(be_ref[...])
    g = -jnp.exp(alog_ref[...]) * _softplus(ae_ref[...] + dtb_ref[...])
    gc = _dot_ones_l(ltri_ref[...], g)
    gt = -jnp.exp(alogt_ref[...]) * _softplus(at_ref[0] + dtbt_ref[...])
    gct = _dot_ones_r(gt, utri_ref[...])
    gl = gc[CHUNK - 1:CHUNK, :]
    eg = jnp.exp(gc)
    kb = kn * beta
    vb = v * beta
    kbe = kb * eg
    qe = qn * eg
    kdec = kn * jnp.exp(gl - gc)
    egl = jnp.exp(gl)
    ri = lax.broadcasted_iota(I32, (CHUNK, CHUNK), 0)
    ci = lax.broadcasted_iota(I32, (CHUNK, CHUNK), 1)
    causal = ri >= ci
    strict = ri > ci
    eye = eye_ref[...]
    heads = range(HEADS)
    hsl = [slice(DH * h, DH * (h + 1)) for h in heads]
    dec = [jnp.exp(jnp.where(causal, gc[:, hsl[h]] - gct[h:h + 1, :], NEG)) for h in heads]
    kh = [kn[:, hsl[h]].astype(BF16) for h in heads]
    p = [-jnp.where(strict, _mm_nt(kb[:, hsl[h]], kh[h]) * dec[h], 0.0) for h in heads]
    attn = [_mm_nt(qn[:, hsl[h]], kh[h]) * dec[h] for h in heads]
    kdt = [_mm_nt(eye, kdec[:, hsl[h]]) for h in heads]
    xx = [jnp.concatenate([vb[:, hsl[h]], kbe[:, hsl[h]]], axis=1) for h in heads]
    for r in range(6):
        xx = [xx[h] + _mm_inv(p[h], xx[h]) for h in heads]
        if r < 5:
            p = [_mm_inv(p[h], p[h]) for h in heads]
    sh = [s_scr[h] for h in heads]
    v_new = [xx[h][:, :DH] - _mm(xx[h][:, DH:], sh[h]) for h in heads]
    outs = [_mm(qe[:, hsl[h]], sh[h]) + _mm(attn[h], v_new[h]) for h in heads]
    for h in heads:
        s_scr[h] = sh[h] * egl[:, hsl[h]] + _mm(kdt[h], v_new[h])
    o = jnp.concatenate(outs, axis=1)
    on = o * lax.rsqrt(_head_sum(o * o, bd) * (1.0 / DH) + EPS) * ng_ref[...]
    z = z_ref[...]
    o_ref[...] = (on * (z * _sigmoid(z))).astype(BF16)

    @pl.when(c == pl.num_programs(0) - 1)
    def _():
        s_out_ref[...] = s_scr[...]


def _gdn_prompt(qkv, a_e, b_e, a_t, z, conv_w, alog_e, dtb_e, alog_t, dtb_t, ng, bd, ltri, utri, eye):
    t = qkv.shape[0]
    row = lambda w: pl.BlockSpec((CHUNK, w), lambda i: (i, 0))
    full = lambda a: pl.BlockSpec(a.shape, lambda i: (0,) * a.ndim)
    return pl.pallas_call(
        _gdn_body,
        grid=(t // CHUNK,),
        in_specs=[row(CONV_DIM), row(HW), row(HW), pl.BlockSpec((1, HEADS, CHUNK), lambda i: (i, 0, 0)), row(HW),
                  full(conv_w), full(alog_e), full(dtb_e), full(alog_t), full(dtb_t), full(ng), full(bd),
                  full(ltri), full(utri), full(eye)],
        out_specs=[row(HW), pl.BlockSpec((HEADS, DH, DH), lambda i: (0, 0, 0))],
        out_shape=[jax.ShapeDtypeStruct((t, HW), BF16), jax.ShapeDtypeStruct((HEADS, DH, DH), F32)],
        scratch_shapes=[pltpu.VMEM((CHUNK + 8, CONV_DIM), F32), pltpu.VMEM((HEADS, DH, DH), F32)],
        compiler_params=_cparams(("arbitrary",)),
        name="gdn_prompt",
    )(qkv, a_e, b_e, a_t, z, conv_w, alog_e, dtb_e, alog_t, dtb_t, ng, bd, ltri, utri, eye)


def _gdn_step_body(xq_ref, xk_ref, xv_ref, cq_ref, ck_ref, cv_ref, a_ref, b_ref, alog_ref, dtb_ref, z_ref, ng_ref,
                   s0_ref, o_ref, s1_ref):
    def conv(x_ref, c_ref):
        acc = x_ref[0, 0] * c_ref[0, 0]
        for i in range(1, CONV_WIDTH):
            acc = acc + x_ref[i, 0] * c_ref[i, 0]
        return acc * _sigmoid(acc)

    q = conv(xq_ref, cq_ref)
    k = conv(xk_ref, ck_ref)
    v = conv(xv_ref, cv_ref)
    qn = q * lax.rsqrt(jnp.sum(q * q, axis=0, keepdims=True) + EPS) * (DH ** -0.5)
    kn = k * lax.rsqrt(jnp.sum(k * k, axis=0, keepdims=True) + EPS)
    beta = _sigmoid(b_ref[0])
    eg = jnp.exp(-jnp.exp(alog_ref[0]) * _softplus(a_ref[0] + dtb_ref[0]))
    ks = jnp.zeros_like(v)
    for d in range(DH):
        ks = ks + kn[d:d + 1, :] * s0_ref[0, d]
    delta = (v - ks * eg) * beta
    o = jnp.zeros_like(v)
    for d in range(DH):
        s_new = s0_ref[0, d] * eg + kn[d:d + 1, :] * delta
        s1_ref[0, d] = s_new
        o = o + qn[d:d + 1, :] * s_new
    on = o * lax.rsqrt(jnp.mean(o * o, axis=0, keepdims=True) + EPS) * ng_ref[...]
    z = z_ref[0]
    o_ref[0] = on * (z * _sigmoid(z))


def _gdn_step(xq, xk, xv, cq, ck, cv, a_t, b_t, alog_t, dtb_t, z_t, ng_t, s0_t):
    b = xq.shape[-1]
    x_spec = pl.BlockSpec((CONV_WIDTH, 1, DH, b), lambda h: (0, h, 0, 0))
    r_spec = pl.BlockSpec((1, 1, b), lambda h: (h, 0, 0))
    s_spec = pl.BlockSpec((1, DH, DH, b), lambda h: (h, 0, 0, 0))
    hd_spec = pl.BlockSpec((1, DH, b), lambda h: (h, 0, 0))
    return pl.pallas_call(
        _gdn_step_body,
        grid=(HEADS,),
        in_specs=[x_spec, x_spec, x_spec, x_spec, x_spec, x_spec, r_spec, r_spec, r_spec, r_spec, hd_spec,
                  pl.BlockSpec((DH, b), lambda h: (0, 0)), s_spec],
        out_specs=[hd_spec, s_spec],
        out_shape=[jax.ShapeDtypeStruct((HEADS, DH, b), F32), jax.ShapeDtypeStruct((HEADS, DH, DH, b), F32)],
        compiler_params=_cparams(("parallel",)),
        name="gdn_step",
    )(xq, xk, xv, cq, ck, cv, a_t, b_t, alog_t, dtb_t, z_t, ng_t, s0_t)


def _count_tiles(sc, n_tiles, pred):
    def tile_hits(kt):
        hit = jnp.where(pred(kt, sc[kt]), 1, 0).astype(I32)
        return jnp.sum(hit.reshape(BK // 8, 8, BQ), axis=0)

    def body(i, cnt):
        k0 = 2 * i
        k1 = jnp.minimum(k0 + 1, n_tiles - 1)
        return cnt + tile_hits(k0) + tile_hits(k1) * jnp.where(k0 + 1 < n_tiles, 1, 0)

    cnt = lax.fori_loop(0, (n_tiles + 1) // 2, body, jnp.zeros((8, BQ), I32))
    return jnp.sum(cnt, axis=0, keepdims=True)


def _count_ge(sc, n_tiles, cand):
    return _count_tiles(sc, n_tiles, lambda kt, s: s >= cand)


def _key_to_f32(key):
    f = pltpu.bitcast(jnp.where(key < 0, key ^ jnp.int32(0x7FFFFFFF), key), F32)
    return jnp.where(key <= INT_MIN + 0x7FFFFF, -jnp.inf, f)


def _greedy_key(count_fn, k, shape, bits):
    lo = jnp.where(count_fn(jnp.zeros(shape, F32)) >= k, 0, INT_MIN).astype(I32)

    def bit(i, lo):
        cand = lo + jnp.left_shift(jnp.int32(1), 30 - i)
        return jnp.where(count_fn(_key_to_f32(cand)) >= k, cand, lo)

    return lax.fori_loop(0, bits, bit, lo)


def _kth_largest(count_fn, k, shape):
    return _key_to_f32(_greedy_key(count_fn, k, shape, 31))


def _kth_largest_bracketed(count_fn, k, lo, hi):
    def unresolved(lo, hi, c_lo):
        gap = hi - lo
        return ((gap > 1) | (gap < 0)) & (c_lo != k)

    def cond(st):
        i, lo, hi, c_lo = st
        return (i < 34) & (jnp.max(jnp.where(unresolved(lo, hi, c_lo), 1, 0)) > 0)

    def body(st):
        i, lo, hi, c_lo = st
        for _ in range(BISECT_STEPS_PER_CHECK):
            mid = lo + lax.shift_right_logical(hi - lo, 1)
            c = count_fn(_key_to_f32(mid))
            take = c >= k
            lo, hi, c_lo = jnp.where(take, mid, lo), jnp.where(take, hi, mid), jnp.where(take, c, c_lo)
        return i + BISECT_STEPS_PER_CHECK, lo, hi, c_lo

    _, lo, _, c_lo = lax.while_loop(cond, body, (jnp.int32(0), lo, hi, count_fn(_key_to_f32(lo))))
    return lo, c_lo


def _dsa_body(kib_ref, qit_ref, wt_ref, qt_ref, k_ref, vt_ref, o_ref, sc, gm, thr, m_s, a_s, lg_s, acc, qz, *,
              topk, idx_bits):
    qb = pl.program_id(0)
    kb = pl.program_id(1)
    q0 = qb * BQ
    n_tiles = (q0 + BQ + BK - 1) // BK
    qpos = q0 + lax.broadcasted_iota(I32, (BK, BQ), 1)
    rowi = lax.broadcasted_iota(I32, (BK, BQ), 0)

    @pl.when(kb == 0)
    def _score():
        w = wt_ref[...]

        def tile(kt, carry):
            ki_t = kib_ref[kt]
            s = jnp.zeros((BK, BQ), F32)
            for h in range(HEADS):
                sh = jnp.dot(ki_t, qit_ref[DH * h:DH * (h + 1), :], preferred_element_type=F32)
                s = s + jnp.maximum(sh, 0.0) * w[h:h + 1, :]
            vis = (kt * BK + rowi) <= qpos
            s = jnp.where(vis, s, -jnp.inf)
            sc[kt] = s
            gm[pl.ds(pl.multiple_of(kt * 8, 8), 8), :] = jnp.max(s.reshape(BK // 8, 8, BQ), axis=0)
            return carry

        gm[...] = jnp.full_like(gm, -jnp.inf)
        lax.fori_loop(0, n_tiles, tile, 0)

        def gm_count(cand):
            hit = jnp.where(gm[...] >= cand, 1, 0).astype(I32)
            return jnp.sum(jnp.sum(hit.reshape(gm.shape[0] // 8, 8, BQ), axis=0), axis=0, keepdims=True)

        k_up = max(topk // (BK // 8), 1)
        lo = _greedy_key(gm_count, topk, (1, BQ), BRACKET_BITS)
        hi = _greedy_key(gm_count, k_up, (1, BQ), BRACKET_BITS) + (1 << (31 - BRACKET_BITS))
        key, c_key = _kth_largest_bracketed(functools.partial(_count_ge, sc, n_tiles), topk, lo, hi)
        t = _key_to_f32(key)
        thr[...] = t
        over = (c_key > topk) & (t > -jnp.inf)

        @pl.when(jnp.max(jnp.where(over, 1, 0)) > 0)
        def _break_ties():
            room = topk - _count_tiles(sc, n_tiles, lambda kt, s: s > t)

            def bit(i, x):
                cand = x + jnp.left_shift(jnp.int32(1), idx_bits - 1 - i)
                below = _count_tiles(sc, n_tiles, lambda kt, s: (s == t) & ((kt * BK + rowi) < cand))
                return jnp.where(below < room, cand, x)

            last_kept = lax.fori_loop(0, idx_bits, bit, jnp.zeros((1, BQ), I32))

            def drop(kt, carry):
                s = sc[kt]
                sc[kt] = jnp.where((s == t) & ((kt * BK + rowi) > last_kept), -jnp.inf, s)
                return carry

            lax.fori_loop(0, n_tiles, drop, 0)

        m_s[...] = jnp.full_like(m_s, NEG)
        acc[...] = jnp.zeros_like(acc)
        zero = jnp.zeros((DH, BQ), BF16)
        for h in range(HEADS):
            qh = qt_ref[DH * h:DH * (h + 1), :]
            qz[h, 0:DH, :] = qh if h % 2 == 0 else zero
            qz[h, DH:2 * DH, :] = zero if h % 2 == 0 else qh

    @pl.when(kb * (KV_TILES_PER_BLOCK * BK) < q0 + BQ)
    def _attend():
        t0 = kb * KV_TILES_PER_BLOCK
        th = thr[...]

        def tile(j, carry):
            gt = t0 + j
            msk = (sc[gt] >= th) & ((gt * BK + rowi) <= qpos)
            bias = jnp.where(msk, 0.0, NEG)
            for h in range(HEADS):
                pair = LANES * (h // 2)
                lg = jnp.dot(k_ref[j, :, pair:pair + LANES], qz[h], preferred_element_type=F32) + bias
                lg_s[h] = lg
                m_old = m_s[h]
                m_new = jnp.maximum(m_old, jnp.max(lg, axis=0, keepdims=True))
                a_s[h] = jnp.exp2(m_old - m_new)
                m_s[h] = m_new
            for h in range(HEADS):
                p = jnp.exp2(lg_s[h] - m_s[h])
                pv = jnp.dot(vt_ref[j, h], p.astype(BF16), preferred_element_type=F32)
                acc[h] = a_s[h] * acc[h] + pv
            return carry

        lax.fori_loop(0, jnp.minimum(KV_TILES_PER_BLOCK, n_tiles - t0), tile, 0)

    @pl.when(kb == pl.num_programs(1) - 1)
    def _finish():
        for h in range(HEADS):
            o_ref[DH * h:DH * (h + 1), :] = acc[h, 0:DH, :] / acc[h, DH:DH + 1, :]


def _dsa_prompt(kib3, qit, wt, qt, k3, vt3, topk):
    nkt = kib3.shape[0]
    t = nkt * BK
    nq = t // BQ
    nkb = -(-nkt // KV_TILES_PER_BLOCK)
    kvb = min(KV_TILES_PER_BLOCK, nkt)
    span = kvb * BK

    def kv_idx(qb, kb):
        return (jnp.minimum(kb, (qb * BQ + BQ - 1) // span), 0, 0)

    return pl.pallas_call(
        functools.partial(_dsa_body, topk=topk, idx_bits=(t - 1).bit_length()),
        grid=(nq, nkb),
        in_specs=[pl.BlockSpec((nkt, BK, DH), lambda qb, kb: (0, 0, 0)),
                  pl.BlockSpec((HW, BQ), lambda qb, kb: (0, qb)),
                  pl.BlockSpec((HEADS, BQ), lambda qb, kb: (0, qb)),
                  pl.BlockSpec((HW, BQ), lambda qb, kb: (0, qb)),
                  pl.BlockSpec((kvb, BK, HW), kv_idx),
                  pl.BlockSpec((kvb, HEADS, V_ROWS, BK), lambda qb, kb: kv_idx(qb, kb) + (0,))],
        out_specs=pl.BlockSpec((HW, BQ), lambda qb, kb: (0, qb)),
        out_shape=jax.ShapeDtypeStruct((HW, t), F32),
        scratch_shapes=[pltpu.VMEM((nkt, BK, BQ), F32), pltpu.VMEM((nkt * 8, BQ), F32), pltpu.VMEM((1, BQ), F32),
                        pltpu.VMEM((HEADS, 1, BQ), F32), pltpu.VMEM((HEADS, 1, BQ), F32),
                        pltpu.VMEM((HEADS, BK, BQ), F32), pltpu.VMEM((HEADS, V_ROWS, BQ), F32),
                        pltpu.VMEM((HEADS, 2 * DH, BQ), BF16)],
        compiler_params=_cparams(("parallel", "arbitrary")),
        name="dsa_prompt",
    )(kib3, qit, wt, qt, k3, vt3)


def _dsa_scores_body(pt_ref, *refs, page):
    del pt_ref
    pp = PAGES_PER_STEP
    ik_refs = refs[0:pp]
    qi_ref, w_ref, sc_ref = refs[pp:]
    pages = jnp.concatenate([r[...] for r in ik_refs], axis=1)
    s8 = _mm3(qi_ref[...], pages)
    srow = jnp.sum(jnp.maximum(s8, 0.0) * w_ref[...][:, 0:1], axis=0, keepdims=True)
    sc_ref[...] = jnp.concatenate([srow[:, i * page:(i + 1) * page] for i in range(pp)], axis=0)


def _dsa_scores(page_table, cik_t, qi_s, w_s):
    b, n_pages = page_table.shape
    page = cik_t.shape[2]
    pp = PAGES_PER_STEP

    def ik_spec(i):
        return pl.BlockSpec((None, DH, page), lambda bb, c, pt: (pt[bb, c * pp + i], 0, 0))

    per_b = lambda r, w: pl.BlockSpec((None, r, w), lambda bb, c, pt: (bb, 0, 0))
    grid_spec = pltpu.PrefetchScalarGridSpec(
        num_scalar_prefetch=1,
        grid=(b, n_pages // pp),
        in_specs=[ik_spec(i) for i in range(pp)] + [per_b(HEADS, DH), per_b(HEADS, page)],
        out_specs=pl.BlockSpec((None, pp, page), lambda bb, c, pt: (bb, c, 0)),
    )
    return pl.pallas_call(
        functools.partial(_dsa_scores_body, page=page),
        grid_spec=grid_spec,
        out_shape=jax.ShapeDtypeStruct((b, n_pages, page), F32),
        compiler_params=_cparams(("parallel", "arbitrary")),
        name="dsa_scores",
    )(page_table, *([cik_t] * pp), qi_s, w_s)


def _dsa_pick_body(sc_ref, qi_ref, w_ref, kin_ref, mask_ref, new_ref, *, topk):
    s_all = sc_ref[...]
    nb, n_pages, page = s_all.shape
    w = w_ref[...][:, :, 0:1]
    s_new = jnp.sum(qi_ref[...] * kin_ref[...], axis=2, keepdims=True)
    s_new = jnp.sum(jnp.maximum(s_new, 0.0) * w, axis=1, keepdims=True)
    total = lambda m: jnp.sum(jnp.sum(m, axis=2, keepdims=True), axis=1, keepdims=True)

    def count(cand):
        return total(jnp.where(s_all >= cand, 1.0, 0.0)) + jnp.where(s_new >= cand, 1.0, 0.0)

    thr = _kth_largest(count, topk, (nb, 1, 1))
    ri = lax.broadcasted_iota(I32, (page, page), 0)
    ci = lax.broadcasted_iota(I32, (page, page), 1)
    upper = jnp.where(ri <= ci, 1.0, 0.0).astype(BF16)
    rp = lax.broadcasted_iota(I32, (n_pages, n_pages), 0)
    cp = lax.broadcasted_iota(I32, (n_pages, n_pages), 1)
    lower_strict = jnp.where(cp < rp, 1.0, 0.0).astype(BF16)
    gt = s_all > thr
    eq = s_all == thr
    eqf = jnp.where(eq, 1.0, 0.0)
    within = jnp.dot(eqf.reshape(nb * n_pages, page).astype(BF16), upper,
                     preferred_element_type=F32).reshape(nb, n_pages, page)
    tot = jnp.broadcast_to(within[:, :, page - 1:page], within.shape).astype(BF16)
    before = jnp.dot(lower_strict, jnp.concatenate([tot[i] for i in range(nb)], axis=1),
                     preferred_element_type=F32)
    cum_eq = within + jnp.stack([before[:, i * page:(i + 1) * page] for i in range(nb)], axis=0)
    room = topk - total(jnp.where(gt, 1.0, 0.0)) - jnp.where(s_new > thr, 1.0, 0.0)
    sel = gt | (eq & (cum_eq <= room))
    new_sel = (s_new > thr) | ((s_new == thr) & (total(eqf) < room))
    mask_ref[...] = jnp.where(sel, 1.0, 0.0)
    new_ref[...] = jnp.where(new_sel, 1.0, 0.0) * jnp.ones(new_ref.shape, F32)


def _dsa_pick(scores, qi_s, w_s, ki_new, topk):
    b, n_pages, page = scores.shape
    full = lambda a: pl.BlockSpec(a.shape, lambda i: (0,) * a.ndim)
    return pl.pallas_call(
        functools.partial(_dsa_pick_body, topk=topk),
        grid=(1,),
        in_specs=[full(scores), full(qi_s), full(w_s), full(ki_new)],
        out_specs=[pl.BlockSpec((b, n_pages, page), lambda i: (0, 0, 0)),
                   pl.BlockSpec((b, 8, page), lambda i: (0, 0, 0))],
        out_shape=[jax.ShapeDtypeStruct((b, n_pages, page), F32), jax.ShapeDtypeStruct((b, 8, page), F32)],
        compiler_params=_cparams(("arbitrary",)),
        name="dsa_pick",
    )(scores, qi_s, w_s, ki_new)


def _dsa_attend_body(pt_ref, *refs, n_chunks):
    del pt_ref
    pp = PAGES_PER_STEP
    k_refs = refs[0:pp]
    v_refs = refs[pp:2 * pp]
    mask_ref, new_ref, q_ref, kn_ref, vn_ref, o_ref, qb, m_s, l_s, acc = refs[2 * pp:]
    c = pl.program_id(1)

    @pl.when(c == 0)
    def _init():
        qb[...] = jnp.broadcast_to(q_ref[...], qb.shape)
        m_s[...] = jnp.full_like(m_s, NEG)
        l_s[...] = jnp.zeros_like(l_s)
        acc[...] = jnp.zeros_like(acc)

    for i in range(pp):
        sel_row = mask_ref[i:i + 1, :] > 0.0
        for h in range(HEADS):
            lg = jnp.sum(k_refs[i][h] * qb[h], axis=0, keepdims=True)
            lg = jnp.where(sel_row, lg, NEG)
            m_old = m_s[h]
            m_new = jnp.maximum(m_old, jnp.max(lg, axis=1, keepdims=True))
            alpha = jnp.exp(m_old - m_new)
            p = jnp.exp(lg - m_new)
            l_s[h] = alpha * l_s[h] + p
            acc[h] = alpha * acc[h] + p * v_refs[i][h]
            m_s[h] = m_new

    @pl.when(c == n_chunks - 1)
    def _finish():
        new_sel = new_ref[0:1, 0:1].reshape(1, 1, 1) > 0.0
        lg_new = jnp.where(new_sel, jnp.sum(q_ref[...] * kn_ref[...], axis=1, keepdims=True), NEG)
        m_old = m_s[...][:, :, 0:1]
        m_fin = jnp.maximum(m_old, lg_new)
        scale = jnp.exp(m_old - m_fin)
        p_new = jnp.exp(lg_new - m_fin)
        den = jnp.sum(l_s[...], axis=2, keepdims=True) * scale + p_new
        o_ref[...] = (jnp.sum(acc[...], axis=2, keepdims=True) * scale + p_new * vn_ref[...]) / den


def _dsa_attend(page_table, ck_t, cv_t, mask, new_sel, q_c, k_c, v_c):
    b, n_pages = page_table.shape
    page = ck_t.shape[3]
    pp = PAGES_PER_STEP
    n_chunks = n_pages // pp

    def kv_spec(i):
        return pl.BlockSpec((None, HEADS, DH, page), lambda bb, c, pt: (pt[bb, c * pp + i], 0, 0, 0))

    col = pl.BlockSpec((None, HEADS, DH, 1), lambda bb, c, pt: (bb, 0, 0, 0))
    grid_spec = pltpu.PrefetchScalarGridSpec(
        num_scalar_prefetch=1,
        grid=(b, n_chunks),
        in_specs=([kv_spec(i) for i in range(pp)] + [kv_spec(i) for i in range(pp)]
                  + [pl.BlockSpec((None, pp, page), lambda bb, c, pt: (bb, c, 0)),
                     pl.BlockSpec((None, 8, page), lambda bb, c, pt: (bb, 0, 0)), col, col, col]),
        out_specs=col,
        scratch_shapes=[pltpu.VMEM((HEADS, DH, page), F32), pltpu.VMEM((HEADS, 1, page), F32),
                        pltpu.VMEM((HEADS, 1, page), F32), pltpu.VMEM((HEADS, DH, page), F32)],
    )
    return pl.pallas_call(
        functools.partial(_dsa_attend_body, n_chunks=n_chunks),
        grid_spec=grid_spec,
        out_shape=jax.ShapeDtypeStruct((b, HEADS, DH, 1), F32),
        compiler_params=_cparams(("parallel", "arbitrary")),
        name="dsa_attend",
    )(page_table, *([ck_t] * pp), *([cv_t] * pp), mask, new_sel, q_c, k_c, v_c)


def _route(logits):
    lane = lax.broadcasted_iota(I32, logits.shape, 1)
    mx = lambda a: jnp.max(a, axis=-1, keepdims=True)
    sm = lambda a: jnp.sum(a, axis=-1, keepdims=True)
    first = lambda hit: jnp.min(jnp.where(hit, lane, LANES), axis=-1, keepdims=True)
    gmask = lane < N_GROUPS
    lg = jnp.where(gmask, logits, NEG)
    gex = jnp.where(gmask, jnp.exp(lg - mx(lg)), 0.0)
    gp = gex / sm(gex)
    g_prob = mx(gp)
    g_idx = first(gmask & (gp == g_prob))
    e_lo = N_GROUPS + EPG * g_idx
    emask = (lane >= e_lo) & (lane < e_lo + EPG)
    le = jnp.where(emask, logits, NEG)
    eex = jnp.where(emask, jnp.exp(le - mx(le)), 0.0)
    ep = jnp.where(emask, eex / sm(eex), -1.0)
    p1 = mx(ep)
    i1 = first(ep == p1)
    ep2 = jnp.where(lane == i1, -1.0, ep)
    p2 = mx(ep2)
    i2 = first(ep2 == p2)
    den = p1 + p2
    return jnp.where(lane == i1, g_prob * p1 / den, 0.0) + jnp.where(lane == i2, g_prob * p2 / den, 0.0)


def _finish_body(x_ref, mix_ref, wo_ref, g_ref, wr1_ref, wr2_ref, wr3_ref, wg_ref, wu_ref, wd_ref,
                 y_ref, h2b, cw, acc, *, exact):
    e = pl.program_id(1)

    @pl.when(e == 0)
    def _():
        if exact:
            proj = jnp.dot(mix_ref[...], wo_ref[...], precision=lax.Precision.HIGHEST, preferred_element_type=F32)
        else:
            proj = jnp.dot(mix_ref[...], wo_ref[...], preferred_element_type=F32)
        y1 = x_ref[...] + proj
        ms = jnp.mean(y1 * y1, axis=-1, keepdims=True)
        h2 = y1 * lax.rsqrt(ms + EPS) * g_ref[...]
        h2b[...] = h2.astype(BF16)
        acc[...] = y1
        a1, a2, a3 = _split3(h2)
        d = lambda a, b_ref: jnp.dot(a, b_ref[...], preferred_element_type=F32)
        logits = (d(a1, wr1_ref) + d(a1, wr2_ref) + d(a2, wr1_ref)
                  + d(a1, wr3_ref) + d(a2, wr2_ref) + d(a3, wr1_ref))
        cw[...] = _route(logits)

    hb = h2b[...]
    cwv = cw[...]
    lane = lax.broadcasted_iota(I32, cwv.shape, 1)
    for j in range(EXPERTS_PER_STEP):
        eid = e * EXPERTS_PER_STEP + j
        gte = jnp.dot(hb, wg_ref[j], preferred_element_type=F32)
        up = jnp.dot(hb, wu_ref[j], preferred_element_type=F32)
        cwe = jnp.sum(jnp.where(lane == N_GROUPS + eid, cwv, 0.0), axis=-1, keepdims=True)
        a = (gte * _sigmoid(gte)) * up * cwe
        acc[...] += jnp.dot(a.astype(BF16), wd_ref[j], preferred_element_type=F32)

    @pl.when(e == pl.num_programs(1) - 1)
    def _():
        y_ref[...] = acc[...]


def _finish(x_all, mix, wo, ffn_g, wr1, wr2, wr3, wg, wu, wd, tm, exact):
    tp = x_all.shape[0]
    eb = EXPERTS_PER_STEP
    row = lambda w: pl.BlockSpec((tm, w), lambda i, e: (i, 0))
    full = lambda a: pl.BlockSpec(a.shape, lambda i, e: (0,) * a.ndim)
    return pl.pallas_call(
        functools.partial(_finish_body, exact=exact),
        grid=(tp // tm, N_EXPERTS // eb),
        in_specs=[row(D_MODEL), row(2 * HW), full(wo), full(ffn_g), full(wr1), full(wr2), full(wr3),
                  pl.BlockSpec((eb, D_MODEL, D_EXPERT), lambda i, e: (e, 0, 0)),
                  pl.BlockSpec((eb, D_MODEL, D_EXPERT), lambda i, e: (e, 0, 0)),
                  pl.BlockSpec((eb, D_EXPERT, D_MODEL), lambda i, e: (e, 0, 0))],
        out_specs=row(D_MODEL),
        out_shape=jax.ShapeDtypeStruct((tp, D_MODEL), F32),
        scratch_shapes=[pltpu.VMEM((tm, D_MODEL), BF16), pltpu.VMEM((tm, LANES), F32),
                        pltpu.VMEM((tm, D_MODEL), F32)],
        compiler_params=_cparams(("parallel", "arbitrary")),
        name="finish_exact" if exact else "finish",
    )(x_all, mix, wo, ffn_g, wr1, wr2, wr3, wg, wu, wd)


def _tile_heads(g):
    return jnp.tile(g.astype(F32), HEADS)[None, :]


def kernel(x_prompt, x_sample, cache_k, cache_v, cache_idx_k, state_gdn, state_conv, page_table, attn_norm_g, w_in,
           conv_w, A_log, dt_bias, gdn_norm_g, q_norm_g, k_norm_g, idx_k_norm_g, w_out, ffn_norm_g, w_router_group,
           w_router_expert, w_gate, w_up, w_down):
    assert w_in.shape[0] == 1 and x_prompt.shape[0] == 1 and x_sample.shape[1] == 1
    t = x_prompt.shape[1]
    nb = x_sample.shape[0]
    assert t % BQ == 0 and t % CHUNK == 0 and BQ == BK and t % TM_FIN == 0

    offs = np.concatenate([[0], np.cumsum(IN_SPLITS)])
    seg = lambda i: w_in[0][:, offs[i]:offs[i + 1]]
    tail_pad = LANES - (DH + 3 * HEADS)
    w_r = jnp.concatenate([seg(0), seg(1), seg(4), seg(5), seg(6), seg(7), seg(8), seg(2), seg(3), seg(9),
                           jnp.zeros((D_MODEL, tail_pad), F32)], axis=1).astype(F32)
    hid = np.arange(HW) // DH
    bd = jnp.asarray(hid[:, None] == hid[None, :], BF16)
    tri = np.arange(CHUNK)
    ltri = jnp.asarray(tri[:, None] >= tri[None, :], BF16)
    utri = jnp.asarray(tri[:, None] <= tri[None, :], BF16)
    eye = jnp.asarray(tri[:, None] == tri[None, :], BF16)

    norm_args = (attn_norm_g.astype(F32), bd, _tile_heads(q_norm_g[0]), _tile_heads(k_norm_g[0]),
                 idx_k_norm_g.astype(F32))
    x_p = x_prompt[0].astype(F32)
    x_s = x_sample[:, 0].astype(F32)
    (qkv, z, q_bf, k_f, k_bf, v_f, v_bf, qi_bf, ki_f, misc) = _inproj(
        x_p, norm_args[0], w_r.astype(BF16), *norm_args[1:], tm=TM_IN, exact=False)
    (qkv_s, z_s, q_s, k_s, _, v_s, _, qi_s, ki_s, misc_s) = _inproj(
        x_s, norm_args[0], w_r, *norm_args[1:], tm=nb, exact=True)
    split_misc = lambda m: (m[:, DH:DH + HEADS], m[:, DH + HEADS:DH + 2 * HEADS],
                            m[:, DH + 2 * HEADS:DH + 3 * HEADS] * ((HEADS * DH) ** -0.5))
    a_p, b_p, wi_p = split_misc(misc)
    a_s, b_s, wi_s = split_misc(misc_s)

    rep = lambda a: jnp.repeat(a, DH, axis=-1)
    alog = A_log[0].astype(F32)
    dtb = dt_bias[0].astype(F32)
    n_chunks = t // CHUNK
    o_g_p, s_p = _gdn_prompt(
        qkv, rep(a_p), rep(b_p), a_p.reshape(n_chunks, CHUNK, HEADS).swapaxes(1, 2), z,
        conv_w[0].astype(F32), rep(alog)[None, :], rep(dtb)[None, :],
        jnp.broadcast_to(alog[:, None], (HEADS, CHUNK)), jnp.broadcast_to(dtb[:, None], (HEADS, CHUNK)),
        _tile_heads(gdn_norm_g[0]), bd, ltri, utri, eye)
    conv_p = qkv[t - (CONV_WIDTH - 1):t]

    ext = jnp.concatenate([state_conv[0].astype(F32), qkv_s[:, None, :]], axis=1)
    ext_t = ext.transpose(1, 2, 0).reshape(CONV_WIDTH, 3, HEADS, DH, nb)
    cw_t = jnp.broadcast_to(conv_w[0].astype(F32).reshape(CONV_WIDTH, 3, HEADS, DH, 1), ext_t.shape)
    row_t = lambda a: jnp.broadcast_to(a, (HEADS, nb)).reshape(HEADS, 1, nb)
    o_g_st, s_st = _gdn_step(
        ext_t[:, 0], ext_t[:, 1], ext_t[:, 2], cw_t[:, 0], cw_t[:, 1], cw_t[:, 2],
        row_t(a_s.T), row_t(b_s.T), row_t(alog[:, None]), row_t(dtb[:, None]),
        z_s.T.reshape(HEADS, DH, nb), jnp.broadcast_to(gdn_norm_g[0].astype(F32)[:, None], (DH, nb)),
        state_gdn[0].astype(F32).transpose(1, 2, 3, 0))
    o_g_s = o_g_st.reshape(HW, nb).T
    s_s = s_st.transpose(3, 0, 1, 2)
    conv_s = ext[:, 1:]

    nkt = t // BK
    vt_aug = jnp.concatenate([v_bf.reshape(nkt, BK, HEADS, DH).transpose(0, 2, 3, 1),
                              jnp.ones((nkt, HEADS, V_ROWS - DH, BK), BF16)], axis=2)
    o_a_t = _dsa_prompt(
        ki_f.astype(BF16).reshape(nkt, BK, DH), qi_bf.T, wi_p.T, q_bf.T,
        k_bf.reshape(nkt, BK, HW), vt_aug, min(TOPK_MAX, t // 4))
    o_a_p = o_a_t.T

    n_pool, page = cache_idx_k.shape[1], cache_idx_k.shape[2]
    past = page_table.shape[1] * page
    pt = page_table.astype(I32)
    qi_h = qi_s.reshape(nb, HEADS, DH)
    wi_b = jnp.broadcast_to(wi_s[:, :, None], (nb, HEADS, page))
    scores = _dsa_scores(pt, cache_idx_k[0].astype(F32).transpose(0, 2, 1), qi_h, wi_b)
    mask, new_sel = _dsa_pick(scores, qi_h, wi_b, ki_s[:, None, :], min(TOPK_MAX, (past + 1) // 4))
    col = lambda a: a.reshape(nb, HEADS, DH, 1)
    o_a_s = _dsa_attend(
        pt, cache_k[0].astype(F32).transpose(0, 2, 3, 1), cache_v[0].astype(F32).transpose(0, 2, 3, 1),
        mask, new_sel, col(q_s), col(k_s), col(v_s)).reshape(nb, HW)

    mix_p = jnp.concatenate([o_g_p, o_a_p.astype(BF16)], axis=1)
    mix_s = jnp.concatenate([o_g_s, o_a_s], axis=1)
    w_router = jnp.concatenate([
        w_router_group[0], w_router_expert[0].transpose(1, 0, 2).reshape(D_MODEL, N_EXPERTS),
        jnp.zeros((D_MODEL, LANES - N_GROUPS - N_EXPERTS), F32)], axis=1).astype(F32)
    wr1 = w_router.astype(BF16)
    wr2 = (w_router - wr1.astype(F32)).astype(BF16)
    wr3 = (w_router - wr1.astype(F32) - wr2.astype(F32)).astype(BF16)
    experts = (w_gate[0].astype(BF16), w_up[0].astype(BF16), w_down[0].astype(BF16))
    y_p = _finish(x_p, mix_p, w_out[0].astype(BF16), ffn_norm_g.astype(F32), wr1, wr2, wr3, *experts,
                  tm=TM_FIN, exact=False)
    y_s = _finish(x_s, mix_s, w_out[0].astype(F32), ffn_norm_g.astype(F32), wr1, wr2, wr3, *experts,
                  tm=nb, exact=True)

    return (y_p[None], y_s[:, None, :],
            k_f.reshape(1, 1, t, HEADS, DH), v_f.reshape(1, 1, t, HEADS, DH), ki_f[None, None],
            s_p[None, None], conv_p[None, None],
            k_s.reshape(1, nb, 1, HEADS, DH), v_s.reshape(1, nb, 1, HEADS, DH),
            ki_s.reshape(1, nb, 1, DH), s_s[None], conv_s[None])
```
